```python
import math
import jax, jax.numpy as jnp
from jax import lax
import numpy as np

D_MODEL = 1024
BATCH = 8
SEQ = 4096
DEPTH = 4

N_ATT_HEADS = 8
HEAD_DIM = 64
ATT_W = N_ATT_HEADS * HEAD_DIM
N_IDX_HEADS = 4
IDX_DIM = HEAD_DIM
INDEX_TOPK_MAX = 256
Q_BLOCK = 128
D_SSM = D_MODEL
SSM_HEAD_DIM = 64
N_SSM_HEADS = D_SSM // SSM_HEAD_DIM
N_GROUPS = 2
D_STATE = 128
CONV_K = 4
CHUNK = 128
CONV_CH = D_SSM + 2 * N_GROUPS * D_STATE
ROPE_THETA = 10000.0
EPS = 1e-6
NEG = -1e30
IN_SIZES = (ATT_W, HEAD_DIM, HEAD_DIM, ATT_W,
            N_IDX_HEADS * IDX_DIM, IDX_DIM, N_IDX_HEADS,
            D_SSM, CONV_CH, N_SSM_HEADS,
            D_MODEL, D_MODEL)
N_IN = sum(IN_SIZES)

kernel_name = 'hybrid_dsa_ssd_gated_merge_adaln'


def _rmsnorm(u, g):
    u32 = u.astype(jnp.float32)
    y = u32 * lax.rsqrt(jnp.mean(u32 * u32, axis=-1, keepdims=True) + EPS)
    return (y * g.astype(jnp.float32)).astype(u.dtype)


def _split_cols(p, sizes):
    offs, acc = [], 0
    for s in sizes[:-1]:
        acc += s
        offs.append(acc)
    return jnp.split(p, offs, axis=-1)


def _rope_tables(L):
    inv = ROPE_THETA ** (-jnp.arange(0, HEAD_DIM, 2, dtype=jnp.float32) / HEAD_DIM)
    ang = jnp.arange(L, dtype=jnp.float32)[:, None] * inv[None, :]
    ang = jnp.concatenate([ang, ang], axis=-1)
    return jnp.cos(ang), jnp.sin(ang)


def _rope(u, cos, sin):
    u32 = u.astype(jnp.float32)
    half = u32.shape[-1] // 2
    rot = jnp.concatenate([-u32[..., half:], u32[..., :half]], axis=-1)
    return (u32 * cos + rot * sin).astype(u.dtype)


def _causal_dwconv(u, w, b):
    out = lax.conv_general_dilated(u, w[:, None, :], window_strides=(1,), padding=((CONV_K - 1, 0),),
                                   dimension_numbers=('NWC', 'WIO', 'NWC'), feature_group_count=u.shape[-1])
    return out + b


def _dsa_attention(q, k, v, qi, ki, wi, topk):
    B, L = q.shape[0], q.shape[1]
    nb = L // Q_BLOCK
    kv = jnp.concatenate([k, v], axis=-1)
    key_pos = jnp.arange(L, dtype=jnp.int32)
    ki32 = ki.astype(jnp.float32)

    def blocks(u):
        return u.reshape((B, nb, Q_BLOCK) + u.shape[2:]).swapaxes(0, 1)

    starts = jnp.arange(nb, dtype=jnp.int32) * Q_BLOCK

    def one_block(inp):
        qb, qib, wb, t0 = inp
        qpos = t0 + jnp.arange(Q_BLOCK, dtype=jnp.int32)
        dots = jnp.einsum('bqhd,bsd->bqhs', qib.astype(jnp.float32), ki32) * IDX_DIM ** -0.5
        score = jnp.einsum('bqhs,bqh->bqs', jax.nn.relu(dots), wb.astype(jnp.float32))
        admissible = key_pos[None, :] <= qpos[:, None]
        score = jnp.where(admissible[None], score, NEG)
        _, sel = lax.top_k(score, topk)
        valid = sel <= qpos[None, :, None]
        kvg = jax.vmap(lambda t, i: t[i])(kv, sel)
        kg, vg = kvg[..., :HEAD_DIM], kvg[..., HEAD_DIM:]
        logits = jnp.einsum('bqhd,bqkd->bqhk', qb.astype(jnp.float32), kg.astype(jnp.float32)) * HEAD_DIM ** -0.5
        logits = jnp.where(valid[:, :, None, :], logits, NEG)
        p = jax.nn.softmax(logits, axis=-1)
        o = jnp.einsum('bqhk,bqkd->bqhd', p, vg.astype(jnp.float32))
        return o.astype(q.dtype)

    out = lax.map(one_block, (blocks(q), blocks(qi), blocks(wi), starts))
    return out.swapaxes(0, 1).reshape(B, L, ATT_W)


def _ssd_chunked(xh, dt, a, bm, cm):
    Bsz, L, H, P = xh.shape
    G, N = bm.shape[2], bm.shape[3]
    R = H // G
    nc = L // CHUNK
    xd = (xh * dt[..., None]).reshape(Bsz, nc, CHUNK, G, R, P)
    adt = (dt * a).reshape(Bsz, nc, CHUNK, G, R).transpose(0, 3, 4, 1, 2)
    a_cs = jnp.cumsum(adt, axis=-1)
    bm = bm.reshape(Bsz, nc, CHUNK, G, N)
    cm = cm.reshape(Bsz, nc, CHUNK, G, N)
    tril = jnp.tril(jnp.ones((CHUNK, CHUNK), dtype=bool))
    seg = a_cs[..., :, None] - a_cs[..., None, :]
    decay = jnp.exp(jnp.where(tril, seg, -jnp.inf))
    cb = jnp.einsum('bclgn,bcsgn->bgcls', cm, bm)
    y_diag = jnp.einsum('bgcls,bgrcls,bcsgrp->bclgrp', cb, decay, xd)
    decay_states = jnp.exp(a_cs[..., -1:] - a_cs)
    states = jnp.einsum('bclgn,bgrcl,bclgrp->bcgrpn', bm, decay_states, xd)
    chunk_decay = jnp.exp(a_cs[..., -1])

    def step(h, inp):
        s, d = inp
        return h * d[..., None, None] + s, h

    h0 = jnp.zeros((Bsz, G, R, P, N), dtype=xd.dtype)
    _, prev = lax.scan(step, h0, (states.transpose(1, 0, 2, 3, 4, 5), chunk_decay.transpose(3, 0, 1, 2)))
    prev = prev.transpose(1, 0, 2, 3, 4, 5)
    y_off = jnp.einsum('bclgn,bcgrpn,bgrcl->bclgrp', cm, prev, jnp.exp(a_cs))
    return (y_diag + y_off).reshape(Bsz, L, H, P)


def _layer(x, c, w_ada, b_ada, norm_g, w_in, conv_w, conv_b, dt_bias, a_log, d_skip, ssm_norm_g,
           w_branch_a, w_branch_s, w_out, cos, sin, topk):
    B, L, _ = x.shape
    mod = jax.nn.silu(c) @ w_ada + b_ada
    shift, scale, gate = jnp.split(mod, 3, axis=-1)
    h = _rmsnorm(x, norm_g) * (1.0 + scale[:, None, :]) + shift[:, None, :]
    proj = h @ w_in
    (q, k, v, g_att, qi, ki, wi, z, xbc, dt_raw, gl_att, gl_ssd) = _split_cols(proj, IN_SIZES)

    q = _rope(q.reshape(B, L, N_ATT_HEADS, HEAD_DIM), cos[:, None, :], sin[:, None, :])
    k = _rope(k, cos, sin)
    qi = _rope(qi.reshape(B, L, N_IDX_HEADS, IDX_DIM), cos[:, None, :], sin[:, None, :])
    ki = _rope(ki, cos, sin)
    wi = wi * N_IDX_HEADS ** -0.5
    o_att = _dsa_attention(q, k, v, qi, ki, wi, topk)
    y_att = (o_att * jax.nn.silu(g_att)) @ w_branch_a

    xbc = jax.nn.silu(_causal_dwconv(xbc, conv_w, conv_b))
    xs, bm, cm = jnp.split(xbc, [D_SSM, D_SSM + N_GROUPS * D_STATE], axis=-1)
    xh = xs.reshape(B, L, N_SSM_HEADS, SSM_HEAD_DIM).astype(jnp.float32)
    dt = jax.nn.softplus(dt_raw.astype(jnp.float32) + dt_bias.astype(jnp.float32))
    a = -jnp.exp(a_log.astype(jnp.float32))
    y = _ssd_chunked(xh, dt, a,
                     bm.reshape(B, L, N_GROUPS, D_STATE).astype(jnp.float32),
                     cm.reshape(B, L, N_GROUPS, D_STATE).astype(jnp.float32))
    y = y + d_skip.astype(jnp.float32)[:, None] * xh
    y = y.reshape(B, L, D_SSM) * jax.nn.silu(z.astype(jnp.float32))
    y = y.reshape(B, L, N_GROUPS, D_SSM // N_GROUPS)
    y = y * lax.rsqrt(jnp.mean(y * y, axis=-1, keepdims=True) + EPS)
    y = (y.reshape(B, L, D_SSM) * ssm_norm_g.astype(jnp.float32)).astype(x.dtype)
    y_ssd = y @ w_branch_s

    merged = jax.nn.sigmoid(gl_att) * y_att + jax.nn.sigmoid(gl_ssd) * y_ssd
    return x + gate[:, None, :] * (merged @ w_out)


def setup_inputs(seed: int = 0) -> dict:
    key = jax.random.key(seed)
    ks = jax.random.split(key, 16)
    f32 = jnp.float32
    x = jax.random.normal(ks[0], (BATCH, SEQ, D_MODEL), f32)
    c = jax.random.normal(ks[1], (BATCH, D_MODEL), f32)
    w_ada = jax.random.normal(ks[2], (DEPTH, D_MODEL, 3 * D_MODEL), f32) * (0.5 * D_MODEL ** -0.5)
    b_ada = jax.random.normal(ks[3], (DEPTH, 3 * D_MODEL), f32) * 0.02
    norm_g = 1.0 + 0.05 * jax.random.normal(ks[4], (DEPTH, D_MODEL), f32)
    w_in = jax.random.normal(ks[5], (DEPTH, D_MODEL, N_IN), f32) * D_MODEL ** -0.5
    conv_w = jax.random.normal(ks[6], (DEPTH, CONV_K, CONV_CH), f32) * CONV_K ** -0.5
    conv_b = jax.random.normal(ks[7], (DEPTH, CONV_CH), f32) * 0.02
    dt0 = jnp.exp(jax.random.uniform(ks[8], (DEPTH, N_SSM_HEADS), f32, math.log(1e-3), math.log(1e-1)))
    dt_bias = dt0 + jnp.log(-jnp.expm1(-dt0))
    a_log = jnp.log(jax.random.uniform(ks[9], (DEPTH, N_SSM_HEADS), f32, 1.0, 16.0))
    d_skip = 1.0 + 0.1 * jax.random.normal(ks[10], (DEPTH, N_SSM_HEADS), f32)
    ssm_norm_g = 1.0 + 0.05 * jax.random.normal(ks[11], (DEPTH, D_SSM), f32)
    w_branch_a = jax.random.normal(ks[12], (DEPTH, ATT_W, D_MODEL), f32) * ATT_W ** -0.5
    w_branch_s = jax.random.normal(ks[13], (DEPTH, D_SSM, D_MODEL), f32) * D_SSM ** -0.5
    w_out = jax.random.normal(ks[14], (DEPTH, D_MODEL, D_MODEL), f32) * D_MODEL ** -0.5
    final_g = 1.0 + 0.05 * jax.random.normal(ks[15], (D_MODEL,), f32)
    return {'x': x, 'c': c, 'w_ada': w_ada, 'b_ada': b_ada, 'norm_g': norm_g, 'w_in': w_in,
            'conv_w': conv_w, 'conv_b': conv_b, 'dt_bias': dt_bias, 'a_log': a_log, 'd_skip': d_skip,
            'ssm_norm_g': ssm_norm_g, 'w_branch_a': w_branch_a, 'w_branch_s': w_branch_s,
            'w_out': w_out, 'final_g': final_g}


def reference(x, c, w_ada, b_ada, norm_g, w_in, conv_w, conv_b, dt_bias, a_log, d_skip,
              ssm_norm_g, w_branch_a, w_branch_s, w_out, final_g):
    L = x.shape[1]
    topk = min(INDEX_TOPK_MAX, L // 4)
    cos, sin = _rope_tables(L)
    for i in range(DEPTH):
        x = _layer(x, c, w_ada[i], b_ada[i], norm_g[i], w_in[i], conv_w[i], conv_b[i], dt_bias[i],
                   a_log[i], d_skip[i], ssm_norm_g[i], w_branch_a[i], w_branch_s[i], w_out[i],
                   cos, sin, topk)
    return _rmsnorm(x, final_g)
```

```python
import functools
import math

import numpy as np
import jax
import jax.numpy as jnp
from jax import lax
from jax.experimental import pallas as pl
from jax.experimental.pallas import tpu as pltpu

D_MODEL = 1024
N_ATT_HEADS = 8
HEAD_DIM = 64
ATT_W = N_ATT_HEADS * HEAD_DIM
N_IDX_HEADS = 4
IDX_DIM = HEAD_DIM
INDEX_TOPK_MAX = 256
D_SSM = D_MODEL
SSM_HEAD_DIM = 64
N_SSM_HEADS = D_SSM // SSM_HEAD_DIM
N_GROUPS = 2
D_STATE = 128
CONV_K = 4
CHUNK = 128
CONV_CH = D_SSM + 2 * N_GROUPS * D_STATE
ROPE_THETA = 10000.0
EPS = 1e-6
NEG = -1e30
IN_SIZES = (ATT_W, HEAD_DIM, HEAD_DIM, ATT_W, N_IDX_HEADS * IDX_DIM, IDX_DIM, N_IDX_HEADS,
            D_SSM, CONV_CH, N_SSM_HEADS, D_MODEL, D_MODEL)

LANES = 128
ROPE_W = 1024
MISC_W = LANES
WI_OFF = N_SSM_HEADS
N_PROJ = ROPE_W + 2 * LANES + ATT_W + MISC_W + D_SSM + CONV_CH + 2 * D_MODEL
VMEM_LIMIT = 56 * 1024 * 1024
INT_MIN = -2 ** 31
NEG_UP = float(np.nextafter(np.float32(NEG), np.float32(0.0)))

F32 = jnp.float32
BF16 = jnp.bfloat16
_NT = (((1,), (1,)), ((), ()))


def _silu(v):
    return v * jax.nn.sigmoid(v)


def _dot(a, b):
    return jnp.dot(a, b, preferred_element_type=F32)


def _dot_exact(a, b):
    return jnp.dot(a, b, preferred_element_type=F32, precision=lax.Precision.HIGHEST)


def _mod_kernel(c_ref, w_ref, b_ref, o_ref):
    o_ref[0] = _dot_exact(_silu(c_ref[...]), w_ref[0]) + b_ref[0]


def _modulation(c, w_ada, b_ada):
    depth, d, n = w_ada.shape
    bsz = c.shape[0]
    tn = 1024
    return pl.pallas_call(
        _mod_kernel,
        grid=(depth, n // tn),
        in_specs=[pl.BlockSpec((bsz, d), lambda i, j: (0, 0)),
                  pl.BlockSpec((1, d, tn), lambda i, j: (i, 0, j)),
                  pl.BlockSpec((1, 1, tn), lambda i, j: (i, 0, j))],
        out_specs=pl.BlockSpec((1, bsz, tn), lambda i, j: (i, 0, j)),
        out_shape=jax.ShapeDtypeStruct((depth, bsz, n), F32),
        compiler_params=pltpu.CompilerParams(vmem_limit_bytes=VMEM_LIMIT),
        name="adaln_modulation",
    )(c, w_ada, b_ada.reshape(depth, 1, n))


def _inproj_kernel(x_ref, scale_ref, shift_ref, g_ref, w_ref, cos_ref, sin_ref,
                   q_ref, kd_ref, kid_ref, qi_ref, vlo_ref, vhi_ref, gatt_ref, misc_ref,
                   z_ref, xbc_ref, gla_ref, gls_ref):
    x = x_ref[...]
    h = x * lax.rsqrt(jnp.mean(x * x, axis=-1, keepdims=True) + EPS) * g_ref[...]
    h = h * (1.0 + scale_ref[0]) + shift_ref[0]
    hb = h.astype(BF16)

    def proj(a, b):
        return _dot(hb, w_ref[:, a:b])

    r = proj(0, ROPE_W)
    cos = cos_ref[...]
    sin = sin_ref[...]
    lane = lax.broadcasted_iota(jnp.int32, cos.shape, 1)
    first_half = (lane % HEAD_DIM) < (HEAD_DIM // 2)
    roped = []
    for gi in range(ROPE_W // LANES):
        u = r[:, gi * LANES:(gi + 1) * LANES]
        rot = jnp.where(first_half, pltpu.roll(u, LANES - HEAD_DIM // 2, 1), pltpu.roll(u, HEAD_DIM // 2, 1))
        roped.append(u * cos + rot * sin)
    q_ref[...] = (jnp.concatenate(roped[0:4], axis=1) * HEAD_DIM ** -0.5).astype(BF16)
    kd_ref[...] = roped[4].astype(BF16)
    kid_ref[...] = roped[5].astype(BF16)
    qi_ref[...] = (jnp.concatenate(roped[6:8], axis=1) * IDX_DIM ** -0.5).astype(BF16)

    o = ROPE_W
    vlo_ref[...] = proj(o, o + LANES).astype(BF16)
    vhi_ref[...] = proj(o + LANES, o + 2 * LANES).astype(BF16)
    o += 2 * LANES
    gatt_ref[...] = proj(o, o + ATT_W)
    o += ATT_W
    misc_ref[...] = proj(o, o + MISC_W)
    o += MISC_W
    z_ref[...] = proj(o, o + D_SSM)
    o += D_SSM
    xbc_ref[...] = proj(o, o + CONV_CH)
    o += CONV_CH
    gla_ref[...] = proj(o, o + D_MODEL)
    o += D_MODEL
    gls_ref[...] = proj(o, o + D_MODEL)


def _in_projection(x2, scale, shift, norm_g, w_r, cos_t, sin_t, seq_len):
    m, d = x2.shape
    tm = 256
    nbl = seq_len // tm
    row = lambda i: (i, 0)
    widths = [(ATT_W, BF16), (LANES, BF16), (LANES, BF16), (N_IDX_HEADS * IDX_DIM, BF16),
              (LANES, BF16), (LANES, BF16), (ATT_W, F32), (MISC_W, F32),
              (D_SSM, F32), (CONV_CH, F32), (D_MODEL, F32), (D_MODEL, F32)]
    return pl.pallas_call(
        _inproj_kernel,
        grid=(m // tm,),
        in_specs=[pl.BlockSpec((tm, d), row),
                  pl.BlockSpec((1, 1, d), lambda i: (i // nbl, 0, 0)),
                  pl.BlockSpec((1, 1, d), lambda i: (i // nbl, 0, 0)),
                  pl.BlockSpec((1, d), lambda i: (0, 0)),
                  pl.BlockSpec((d, N_PROJ), lambda i: (0, 0)),
                  pl.BlockSpec((tm, LANES), lambda i: (i % nbl, 0)),
                  pl.BlockSpec((tm, LANES), lambda i: (i % nbl, 0))],
        out_specs=[pl.BlockSpec((tm, w), row) for w, _ in widths],
        out_shape=[jax.ShapeDtypeStruct((m, w), dt) for w, dt in widths],
        compiler_params=pltpu.CompilerParams(vmem_limit_bytes=VMEM_LIMIT),
        name="in_projection",
    )(x2, scale, shift, norm_g, w_r, cos_t, sin_t)


def _attn_kernel(q_ref, qi_ref, misc_ref, gatt_ref, kd_ref, kid_ref, vlo_ref, vhi_ref, o_ref,
                 sc_ref, m_ref, l_ref, acc_ref, *, qb, kc_w, topk, idx_bits):
    i = pl.program_id(1)
    q0 = i * qb
    nkc = lax.div(q0 + qb + kc_w - 1, kc_w)
    lane = lax.broadcasted_iota(jnp.int32, (qb, LANES), 1)
    lo = lane < HEAD_DIM
    kf = float(topk)

    def head_operand(ref, h):
        grp = ref[:, (h // 2) * LANES:(h // 2 + 1) * LANES].astype(F32)
        return jnp.where(lo if h % 2 == 0 else jnp.logical_not(lo), grp, 0.0).astype(BF16)

    def chunk_start(kc):
        return pl.multiple_of(kc * kc_w, kc_w)

    qim = [head_operand(qi_ref, j) for j in range(N_IDX_HEADS)]
    misc = misc_ref[...]
    wj = [misc[:, WI_OFF + j:WI_OFF + j + 1] * N_IDX_HEADS ** -0.5 for j in range(N_IDX_HEADS)]
    qpos = q0 + lax.broadcasted_iota(jnp.int32, (qb, 1), 0)

    def idx_body(kc, carry):
        k0 = chunk_start(kc)
        kid = kid_ref[pl.ds(k0, kc_w), :]
        s = jnp.zeros((qb, kc_w), F32)
        for j in range(N_IDX_HEADS):
            d = lax.dot_general(qim[j], kid, _NT, preferred_element_type=F32)
            s = s + jnp.maximum(d, 0.0) * wj[j]
        kpos = k0 + lax.broadcasted_iota(jnp.int32, (1, kc_w), 1)
        sc_ref[:, pl.ds(k0, kc_w)] = jnp.where(kpos <= qpos, s, NEG)
        return carry

    lax.fori_loop(0, nkc, idx_body, 0)

    def count(pred):
        def body(kc, acc):
            k0 = chunk_start(kc)
            ones = jnp.where(pred(sc_ref[:, pl.ds(k0, kc_w)], k0), 1.0, 0.0)
            part = ones[:, 0:LANES]
            for t in range(1, kc_w // LANES):
                part = part + ones[:, t * LANES:(t + 1) * LANES]
            return acc + part
        acc = lax.fori_loop(0, nkc, body, jnp.zeros((qb, LANES), F32))
        return jnp.sum(acc, axis=-1, keepdims=True)

    def ordinal_to_float(c):
        bits = jnp.where(c >= 0, c, jnp.int32(INT_MIN) - c)
        return pltpu.bitcast(bits, F32)

    cnt0 = count(lambda ch, k0: ch >= 0.0)
    ncols = (nkc * kc_w).astype(F32)
    nonneg = cnt0 >= kf
    prefix0 = jnp.where(nonneg, 0, jnp.int32(INT_MIN))
    cnt_at0 = jnp.where(nonneg, cnt0, ncols)

    def bit_body(b, carry):
        prefix, cnt_at = carry
        cand = prefix | lax.shift_left(jnp.int32(1), 30 - b)
        cf = ordinal_to_float(cand)
        cnt = count(lambda ch, k0: ch >= cf)
        ok = cnt >= kf
        return jnp.where(ok, cand, prefix), jnp.where(ok, cnt, cnt_at)

    prefix, cnt_at = lax.fori_loop(0, 31, bit_body, (prefix0, cnt_at0))
    thr = jnp.where(prefix == jnp.int32(INT_MIN), -jnp.inf, ordinal_to_float(prefix))

    tied = jnp.logical_and(cnt_at > kf, thr > NEG)
    any_tied = jnp.max(jnp.where(tied, 1.0, 0.0)) > 0.5

    @pl.when(any_tied)
    def _():
        want = kf - count(lambda ch, k0: ch > thr)

        def kidx(k0):
            return k0 + lax.broadcasted_iota(jnp.int32, (1, kc_w), 1)

        def jbit(b, last):
            cand = last | lax.shift_left(jnp.int32(1), idx_bits - 1 - b)
            before = count(lambda ch, k0: jnp.logical_and(kidx(k0) < cand, ch == thr))
            return jnp.where(before < want, cand, last)

        last = lax.fori_loop(0, idx_bits, jbit, jnp.zeros((qb, 1), jnp.int32))

        def drop(kc, carry):
            k0 = chunk_start(kc)
            ch = sc_ref[:, pl.ds(k0, kc_w)]
            dropped = jnp.logical_and(kidx(k0) > last, ch == thr)
            sc_ref[:, pl.ds(k0, kc_w)] = jnp.where(dropped, NEG, ch)
            return carry

        lax.fori_loop(0, nkc, drop, 0)

    thr = jnp.maximum(thr, NEG_UP)

    m_ref[...] = jnp.full(m_ref.shape, NEG, F32)
    l_ref[...] = jnp.zeros(l_ref.shape, F32)
    acc_ref[...] = jnp.zeros(acc_ref.shape, F32)
    qm = [head_operand(q_ref, h) for h in range(N_ATT_HEADS)]

    def att_body(kc, carry):
        k0 = chunk_start(kc)
        sel = sc_ref[:, pl.ds(k0, kc_w)] >= thr
        kd = kd_ref[pl.ds(k0, kc_w), :]
        vs = (vlo_ref[pl.ds(k0, kc_w), :], vhi_ref[pl.ds(k0, kc_w), :])
        for g in range(N_ATT_HEADS // 2):
            pv = None
            alphas = []
            for half in range(2):
                h = 2 * g + half
                s = lax.dot_general(qm[h], kd, _NT, preferred_element_type=F32)
                s = jnp.where(sel, s, NEG)
                m_old = m_ref[h]
                m_new = jnp.maximum(m_old, jnp.max(s, axis=-1, keepdims=True))
                alpha = jnp.exp(m_old - m_new)
                p = jnp.exp(s - m_new)
                l_ref[h] = alpha * l_ref[h] + jnp.sum(p, axis=-1, keepdims=True)
                m_ref[h] = m_new
                t = _dot(p.astype(BF16), vs[half])
                pv = t if pv is None else pv + t
                alphas.append(alpha)
            acc_ref[g] = acc_ref[g] * jnp.where(lo, alphas[0], alphas[1]) + pv
        return carry

    lax.fori_loop(0, nkc, att_body, 0)

    outs = []
    for g in range(N_ATT_HEADS // 2):
        outs.append(acc_ref[g] * jnp.where(lo, 1.0 / l_ref[2 * g], 1.0 / l_ref[2 * g + 1]))
    o_ref[...] = (jnp.concatenate(outs, axis=1) * _silu(gatt_ref[...])).astype(BF16)


def _attention(q, qi, misc, gatt, kd, kid, vlo, vhi, bsz, seq_len):
    m = q.shape[0]
    qb = min(256, seq_len)
    kc_w = min(512, seq_len)
    topk = min(INDEX_TOPK_MAX, seq_len // 4)
    assert seq_len % qb == 0 and seq_len % kc_w == 0 and kc_w >= topk
    nqb = seq_len // qb
    qrow = lambda b, i: (b * nqb + i, 0)
    kv = lambda b, i: (b, 0)
    kern = functools.partial(_attn_kernel, qb=qb, kc_w=kc_w, topk=topk,
                             idx_bits=max(1, int(math.ceil(math.log2(seq_len)))))
    return pl.pallas_call(
        kern,
        grid=(bsz, nqb),
        in_specs=[pl.BlockSpec((qb, ATT_W), qrow),
                  pl.BlockSpec((qb, N_IDX_HEADS * IDX_DIM), qrow),
                  pl.BlockSpec((qb, MISC_W), qrow),
                  pl.BlockSpec((qb, ATT_W), qrow),
                  pl.BlockSpec((seq_len, LANES), kv),
                  pl.BlockSpec((seq_len, LANES), kv),
                  pl.BlockSpec((seq_len, LANES), kv),
                  pl.BlockSpec((seq_len, LANES), kv)],
        out_specs=pl.BlockSpec((qb, ATT_W), qrow),
        out_shape=jax.ShapeDtypeStruct((m, ATT_W), BF16),
        scratch_shapes=[pltpu.VMEM((qb, seq_len), F32),
                        pltpu.VMEM((N_ATT_HEADS, qb, 1), F32),
                        pltpu.VMEM((N_ATT_HEADS, qb, 1), F32),
                        pltpu.VMEM((N_ATT_HEADS // 2, qb, LANES), F32)],
        compiler_params=pltpu.CompilerParams(vmem_limit_bytes=VMEM_LIMIT,
                                             dimension_semantics=("arbitrary", "arbitrary")),
        name="dsa_attention",
    )(q, qi, misc, gatt, kd, kid, vlo, vhi)


def _ssd_kernel(xbc_ref, misc_ref, z_ref, cw_ref, cb_ref, dtb_ref, alog_ref, dsk_ref, ng_ref, e_ref,
                y_ref, ext_ref, st_ref):
    c = pl.program_id(1)
    gw = D_SSM // N_GROUPS
    tail = 8

    @pl.when(c == 0)
    def _():
        ext_ref[0:tail, :] = jnp.zeros((tail, CONV_CH), F32)
        st_ref[...] = jnp.zeros(st_ref.shape, F32)

    ext_ref[tail:tail + CHUNK, :] = xbc_ref[...]
    conv = cb_ref[...]
    for j in range(CONV_K):
        conv = conv + ext_ref[tail - j:tail - j + CHUNK, :] * cw_ref[CONV_K - 1 - j:CONV_K - j, :]
    ext_ref[0:tail, :] = ext_ref[CHUNK:CHUNK + tail, :]
    u = _silu(conv)
    xs = u[:, 0:D_SSM]
    bm = u[:, D_SSM:D_SSM + N_GROUPS * D_STATE]
    cm = u[:, D_SSM + N_GROUPS * D_STATE:]

    row = lax.broadcasted_iota(jnp.int32, (CHUNK, LANES), 0)
    lane = lax.broadcasted_iota(jnp.int32, (CHUNK, LANES), 1)
    tril = row >= lane
    lo = lane < SSM_HEAD_DIM

    dtr = misc_ref[...] + dtb_ref[...]
    softplus = jnp.maximum(dtr, 0.0) + jnp.log1p(jnp.exp(-jnp.abs(dtr)))
    dtv = jnp.where(lane < N_SSM_HEADS, softplus, 0.0)
    adt = dtv * (-jnp.exp(alog_ref[...]))
    acs = _dot_exact(jnp.where(tril, 1.0, 0.0), adt)
    acs_t = acs.T
    expand = e_ref[...]
    dt_e = _dot_exact(dtv, expand)
    acs_e = _dot_exact(acs, expand)
    last_e = acs_e[CHUNK - 1:CHUNK, :]
    xd = xs * dt_e
    w_state = (xd * jnp.exp(last_e - acs_e)).astype(BF16)
    decay_in = jnp.exp(acs_e)
    chunk_decay = jnp.exp(last_e)

    ys = []
    for g in range(N_GROUPS):
        bm_g = bm[:, g * D_STATE:(g + 1) * D_STATE]
        cmb = cm[:, g * D_STATE:(g + 1) * D_STATE].astype(BF16)
        cb = lax.dot_general(cmb, bm_g.astype(BF16), _NT, preferred_element_type=F32)
        st = st_ref[g]
        y_off = _dot(cmb, st.astype(BF16)) * decay_in[:, g * gw:(g + 1) * gw]
        st_ref[g] = st * chunk_decay[:, g * gw:(g + 1) * gw] + _dot(bm_g.T.astype(BF16), w_state[:, g * gw:(g + 1) * gw])
        for p in range(gw // LANES):
            pair = g * (gw // LANES) + p
            xp = xd[:, pair * LANES:(pair + 1) * LANES]
            acc = y_off[:, p * LANES:(p + 1) * LANES]
            for half in range(2):
                h = 2 * pair + half
                seg = acs[:, h:h + 1] - acs_t[h:h + 1, :]
                decay = jnp.exp(jnp.where(tril, seg, -jnp.inf))
                xh = jnp.where(lo if half == 0 else jnp.logical_not(lo), xp, 0.0).astype(BF16)
                acc = acc + _dot((cb * decay).astype(BF16), xh)
            ys.append(acc)
    y = jnp.concatenate(ys, axis=1) + dsk_ref[...] * xs
    y = y * _silu(z_ref[...])
    normed = []
    for g in range(N_GROUPS):
        yg = y[:, g * gw:(g + 1) * gw]
        normed.append(yg * lax.rsqrt(jnp.mean(yg * yg, axis=-1, keepdims=True) + EPS))
    y_ref[...] = (jnp.concatenate(normed, axis=1) * ng_ref[...]).astype(BF16)


def _ssd(xbc, misc, z, conv_w, conv_b, dt_bias, a_log, d_skip, ssm_norm_g, expand, bsz, seq_len):
    m = xbc.shape[0]
    nc = seq_len // CHUNK
    rowc = lambda b, c: (b * nc + c, 0)
    const = lambda b, c: (0, 0)
    pad = lambda v: jnp.pad(v, (0, LANES - v.shape[0])).reshape(1, LANES)
    return pl.pallas_call(
        _ssd_kernel,
        grid=(bsz, nc),
        in_specs=[pl.BlockSpec((CHUNK, CONV_CH), rowc),
                  pl.BlockSpec((CHUNK, MISC_W), rowc),
                  pl.BlockSpec((CHUNK, D_SSM), rowc),
                  pl.BlockSpec((CONV_K, CONV_CH), const),
                  pl.BlockSpec((1, CONV_CH), const),
                  pl.BlockSpec((1, LANES), const),
                  pl.BlockSpec((1, LANES), const),
                  pl.BlockSpec((1, D_SSM), const),
                  pl.BlockSpec((1, D_SSM), const),
                  pl.BlockSpec((LANES, D_SSM), const)],
        out_specs=pl.BlockSpec((CHUNK, D_SSM), rowc),
        out_shape=jax.ShapeDtypeStruct((m, D_SSM), BF16),
        scratch_shapes=[pltpu.VMEM((CHUNK + 8, CONV_CH), F32),
                        pltpu.VMEM((N_GROUPS, D_STATE, D_SSM // N_GROUPS), F32)],
        compiler_params=pltpu.CompilerParams(vmem_limit_bytes=VMEM_LIMIT,
                                             dimension_semantics=("arbitrary", "arbitrary")),
        name="ssd_mixer",
    )(xbc, misc, z, conv_w, conv_b.reshape(1, CONV_CH), pad(dt_bias), pad(a_log),
      jnp.repeat(d_skip, SSM_HEAD_DIM).reshape(1, D_SSM), ssm_norm_g.reshape(1, D_SSM), expand)


def _merge_kernel(x_ref, oa_ref, y_ref, gla_ref, gls_ref, gate_ref, wa_ref, ws_ref, wo_ref, fg_ref, o_ref,
                  *, final_norm):
    y_att = _dot(oa_ref[...], wa_ref[...])
    y_ssd = _dot(y_ref[...], ws_ref[...])
    merged = jax.nn.sigmoid(gla_ref[...]) * y_att + jax.nn.sigmoid(gls_ref[...]) * y_ssd
    out = x_ref[...] + gate_ref[0] * _dot(merged.astype(BF16), wo_ref[...])
    if final_norm:
        out = out * lax.rsqrt(jnp.mean(out * out, axis=-1, keepdims=True) + EPS) * fg_ref[...]
    o_ref[...] = out


def _merge(x2, oa, y, gla, gls, gate, w_a, w_s, w_o, final_g, seq_len, final_norm):
    m, d = x2.shape
    tm = 512
    nbl = seq_len // tm
    row = lambda i: (i, 0)
    const = lambda i: (0, 0)
    return pl.pallas_call(
        functools.partial(_merge_kernel, final_norm=final_norm),
        grid=(m // tm,),
        in_specs=[pl.BlockSpec((tm, d), row),
                  pl.BlockSpec((tm, ATT_W), row),
                  pl.BlockSpec((tm, D_SSM), row),
                  pl.BlockSpec((tm, d), row),
                  pl.BlockSpec((tm, d), row),
                  pl.BlockSpec((1, 1, d), lambda i: (i // nbl, 0, 0)),
                  pl.BlockSpec((ATT_W, d), const),
                  pl.BlockSpec((D_SSM, d), const),
                  pl.BlockSpec((d, d), const),
                  pl.BlockSpec((1, d), const)],
        out_specs=pl.BlockSpec((tm, d), row),
        out_shape=jax.ShapeDtypeStruct((m, d), F32),
        compiler_params=pltpu.CompilerParams(vmem_limit_bytes=VMEM_LIMIT),
        name="merge_out_projection",
    )(x2, oa, y, gla, gls, gate, w_a, w_s, w_o, final_g)


def _relayout_w_in(w_in):
    offs = np.cumsum((0,) + IN_SIZES)
    seg = [w_in[..., int(offs[i]):int(offs[i + 1])] for i in range(len(IN_SIZES))]
    q, k, v, g_att, qi, ki, wi, z, xbc, dt, gla, gls = seg
    zeros = lambda n: jnp.zeros(w_in.shape[:-1] + (n,), w_in.dtype)
    cols = [q, k, k, ki, ki, qi,
            v, zeros(HEAD_DIM), zeros(HEAD_DIM), v,
            g_att, dt, wi, zeros(MISC_W - N_SSM_HEADS - N_IDX_HEADS), z, xbc, gla, gls]
    return jnp.concatenate(cols, axis=-1).astype(BF16)


def _rope_tables(seq_len):
    inv = ROPE_THETA ** (-jnp.arange(0, HEAD_DIM, 2, dtype=F32) / HEAD_DIM)
    ang = jnp.arange(seq_len, dtype=F32)[:, None] * inv[None, :]
    cos, sin = jnp.cos(ang), jnp.sin(ang)
    return (jnp.concatenate([cos, cos, cos, cos], axis=-1),
            jnp.concatenate([-sin, sin, -sin, sin], axis=-1))


def kernel(x, c, w_ada, b_ada, norm_g, w_in, conv_w, conv_b, dt_bias, a_log, d_skip, ssm_norm_g,
           w_branch_a, w_branch_s, w_out, final_g):
    bsz, seq_len, d = x.shape
    depth = w_in.shape[0]
    assert d == D_MODEL and seq_len % CHUNK == 0
    mod = _modulation(c, w_ada, b_ada)
    w_r = _relayout_w_in(w_in)
    w_a = w_branch_a.astype(BF16)
    w_s = w_branch_s.astype(BF16)
    w_o = w_out.astype(BF16)
    cos_t, sin_t = _rope_tables(seq_len)
    expand = (jnp.arange(LANES)[:, None] == (jnp.arange(D_SSM) // SSM_HEAD_DIM)[None, :]).astype(F32)
    fg = final_g.reshape(1, d)

    x2 = x.reshape(bsz * seq_len, d)
    for i in range(depth):
        shift = mod[i, :, 0:d].reshape(bsz, 1, d)
        scale = mod[i, :, d:2 * d].reshape(bsz, 1, d)
        gate = mod[i, :, 2 * d:].reshape(bsz, 1, d)
        (q, kd, kid, qi, vlo, vhi, gatt, misc, z, xbc, gla, gls) = _in_projection(
            x2, scale, shift, norm_g[i].reshape(1, d), w_r[i], cos_t, sin_t, seq_len)
        oa = _attention(q, qi, misc, gatt, kd, kid, vlo, vhi, bsz, seq_len)
        y = _ssd(xbc, misc, z, conv_w[i], conv_b[i], dt_bias[i], a_log[i], d_skip[i], ssm_norm_g[i],
                 expand, bsz, seq_len)
        x2 = _merge(x2, oa, y, gla, gls, gate, w_a[i], w_s[i], w_o[i], fg, seq_len, i == depth - 1)
    return x2.reshape(bsz, seq_len, d)
```

```python
import functools
import math

import numpy as np
import jax
import jax.numpy as jnp
from jax import lax
from jax.experimental import pallas as pl
from jax.experimental.pallas import tpu as pltpu

D_MODEL = 1024
N_ATT_HEADS = 8
HEAD_DIM = 64
ATT_W = N_ATT_HEADS * HEAD_DIM
N_IDX_HEADS = 4
IDX_DIM = HEAD_DIM
INDEX_TOPK_MAX = 256
D_SSM = D_MODEL
SSM_HEAD_DIM = 64
N_SSM_HEADS = D_SSM // SSM_HEAD_DIM
N_GROUPS = 2
D_STATE = 128
CONV_K = 4
CHUNK = 128
CONV_CH = D_SSM + 2 * N_GROUPS * D_STATE
ROPE_THETA = 10000.0
EPS = 1e-6
NEG = -1e30
IN_SIZES = (ATT_W, HEAD_DIM, HEAD_DIM, ATT_W, N_IDX_HEADS * IDX_DIM, IDX_DIM, N_IDX_HEADS,
            D_SSM, CONV_CH, N_SSM_HEADS, D_MODEL, D_MODEL)

LANES = 128
ROPE_W = 1024
MISC_W = LANES
WI_OFF = N_SSM_HEADS
LOG2E = math.log2(math.e)
N_PROJ = ROPE_W + LANES + ATT_W + MISC_W + D_SSM + CONV_CH + 2 * D_MODEL
VMEM_LIMIT = 56 * 1024 * 1024
INT_MIN = -2 ** 31
NEG_UP = float(np.nextafter(np.float32(NEG), np.float32(0.0)))

F32 = jnp.float32
BF16 = jnp.bfloat16
_NT = (((1,), (1,)), ((), ()))


def _silu(v):
    return v * jax.nn.sigmoid(v)


def _dot(a, b):
    return jnp.dot(a, b, preferred_element_type=F32)


def _dot_exact(a, b):
    return jnp.dot(a, b, preferred_element_type=F32, precision=lax.Precision.HIGHEST)


def _mod_kernel(c_ref, w_ref, b_ref, o_ref):
    o_ref[0] = _dot_exact(_silu(c_ref[...]), w_ref[0]) + b_ref[0]


def _modulation(c, w_ada, b_ada):
    depth, d, n = w_ada.shape
    bsz = c.shape[0]
    tn = 1024
    return pl.pallas_call(
        _mod_kernel,
        grid=(depth, n // tn),
        in_specs=[pl.BlockSpec((bsz, d), lambda i, j: (0, 0)),
                  pl.BlockSpec((1, d, tn), lambda i, j: (i, 0, j)),
                  pl.BlockSpec((1, 1, tn), lambda i, j: (i, 0, j))],
        out_specs=pl.BlockSpec((1, bsz, tn), lambda i, j: (i, 0, j)),
        out_shape=jax.ShapeDtypeStruct((depth, bsz, n), F32),
        compiler_params=pltpu.CompilerParams(vmem_limit_bytes=VMEM_LIMIT),
        name="adaln_modulation",
    )(c, w_ada, b_ada.reshape(depth, 1, n))


def _inproj_kernel(x_ref, scale_ref, shift_ref, g_ref, w_ref, cos_ref, sin_ref,
                   q_ref, kd_ref, kid_ref, qi_ref, vd_ref, gatt_ref, misc_ref,
                   z_ref, xbc_ref, gla_ref, gls_ref):
    x = x_ref[...]
    h = x * lax.rsqrt(jnp.mean(x * x, axis=-1, keepdims=True) + EPS) * g_ref[...]
    h = h * (1.0 + scale_ref[0]) + shift_ref[0]
    hb = h.astype(BF16)

    def proj(a, b):
        return _dot(hb, w_ref[:, a:b])

    r = proj(0, ROPE_W)
    cos = cos_ref[...]
    sin = sin_ref[...]
    lane = lax.broadcasted_iota(jnp.int32, cos.shape, 1)
    first_half = (lane % HEAD_DIM) < (HEAD_DIM // 2)
    roped = []
    for gi in range(ROPE_W // LANES):
        u = r[:, gi * LANES:(gi + 1) * LANES]
        rot = jnp.where(first_half, pltpu.roll(u, LANES - HEAD_DIM // 2, 1), pltpu.roll(u, HEAD_DIM // 2, 1))
        roped.append(u * cos + rot * sin)
    q_ref[...] = (jnp.concatenate(roped[0:4], axis=1) * (HEAD_DIM ** -0.5 * LOG2E)).astype(BF16)
    kd_ref[...] = roped[4].astype(BF16)
    kid_ref[...] = roped[5].astype(BF16)
    qi_ref[...] = (jnp.concatenate(roped[6:8], axis=1) * IDX_DIM ** -0.5).astype(BF16)

    o = ROPE_W
    vd_ref[...] = proj(o, o + LANES).astype(BF16)
    o += LANES
    gatt_ref[...] = proj(o, o + ATT_W)
    o += ATT_W
    misc_ref[...] = proj(o, o + MISC_W)
    o += MISC_W
    z_ref[...] = proj(o, o + D_SSM)
    o += D_SSM
    xbc_ref[...] = proj(o, o + CONV_CH)
    o += CONV_CH
    gla_ref[...] = proj(o, o + D_MODEL)
    o += D_MODEL
    gls_ref[...] = proj(o, o + D_MODEL)


def _in_projection(x2, scale, shift, norm_g, w_r, cos_t, sin_t, seq_len):
    m, d = x2.shape
    tm = 256
    nbl = seq_len // tm
    row = lambda i: (i, 0)
    widths = [(ATT_W, BF16), (LANES, BF16), (LANES, BF16), (N_IDX_HEADS * IDX_DIM, BF16),
              (LANES, BF16), (ATT_W, F32), (MISC_W, F32),
              (D_SSM, F32), (CONV_CH, F32), (D_MODEL, F32), (D_MODEL, F32)]
    return pl.pallas_call(
        _inproj_kernel,
        grid=(m // tm,),
        in_specs=[pl.BlockSpec((tm, d), row),
                  pl.BlockSpec((1, 1, d), lambda i: (i // nbl, 0, 0)),
                  pl.BlockSpec((1, 1, d), lambda i: (i // nbl, 0, 0)),
                  pl.BlockSpec((1, d), lambda i: (0, 0)),
                  pl.BlockSpec((d, N_PROJ), lambda i: (0, 0)),
                  pl.BlockSpec((tm, LANES), lambda i: (i % nbl, 0)),
                  pl.BlockSpec((tm, LANES), lambda i: (i % nbl, 0))],
        out_specs=[pl.BlockSpec((tm, w), row) for w, _ in widths],
        out_shape=[jax.ShapeDtypeStruct((m, w), dt) for w, dt in widths],
        compiler_params=pltpu.CompilerParams(vmem_limit_bytes=VMEM_LIMIT),
        name="in_projection",
    )(x2, scale, shift, norm_g, w_r, cos_t, sin_t)


def _attn_kernel(q_ref, qi_ref, misc_ref, gatt_ref, kd_ref, kid_ref, vd_ref, o_ref,
                 sc_ref, qs_ref, qis_ref, sa_ref, sb_ref, p_ref, m_ref, l_ref, acc_ref,
                 *, qb, kc_w, topk, idx_bits):
    i = pl.program_id(1)
    q0 = i * qb
    nkc = lax.div(q0 + qb + kc_w - 1, kc_w)
    lane = lax.broadcasted_iota(jnp.int32, (qb, LANES), 1)
    lo = lane < HEAD_DIM
    kf = float(topk)
    n_lg = kc_w // LANES
    rb = 64

    def stack_heads(dst_ref, src_ref, n_heads):
        for h in range(n_heads):
            grp = src_ref[:, (h // 2) * LANES:(h // 2 + 1) * LANES].astype(F32)
            keep = lo if h % 2 == 0 else jnp.logical_not(lo)
            dst_ref[h * qb:(h + 1) * qb, :] = jnp.where(keep, grp, 0.0).astype(BF16)

    def chunk_start(kc):
        return pl.multiple_of(kc * kc_w, kc_w)

    stack_heads(qis_ref, qi_ref, N_IDX_HEADS)
    n_blk = qb // LANES
    blk_lane = lax.broadcasted_iota(jnp.int32, (1, LANES), 1)
    w_rows = []
    for jb in range(n_blk):
        misc_t = misc_ref[jb * LANES:(jb + 1) * LANES, :].T
        w_rows.append([misc_t[WI_OFF + h:WI_OFF + h + 1, :] * N_IDX_HEADS ** -0.5 for h in range(N_IDX_HEADS)])

    def idx_body(kc, masked):
        k0 = chunk_start(kc)
        d = lax.dot_general(kid_ref[pl.ds(k0, kc_w), :], qis_ref[...], _NT, preferred_element_type=F32)
        for jb in range(n_blk):
            s = None
            for h in range(N_IDX_HEADS):
                c0 = h * qb + jb * LANES
                term = jnp.maximum(d[:, c0:c0 + LANES], 0.0) * w_rows[jb][h]
                s = term if s is None else s + term
            if masked:
                kpos = k0 + lax.broadcasted_iota(jnp.int32, (kc_w, 1), 0)
                s = jnp.where(kpos <= q0 + jb * LANES + blk_lane, s, NEG)
            sc_ref[jb, pl.ds(k0, kc_w), :] = s

    def idx_full(kc, carry):
        idx_body(kc, False)
        return carry

    lax.fori_loop(0, nkc - 1, idx_full, 0)
    idx_body(nkc - 1, True)

    def count_below(cand):
        cand_b = [jnp.broadcast_to(cand[jb:jb + 1, :], (rb, LANES)) for jb in range(n_blk)]

        def body(kc, accs):
            k0 = chunk_start(kc)
            out = []
            for jb in range(n_blk):
                acc = accs[jb]
                for r in range(kc_w // rb):
                    diff = sc_ref[jb, pl.ds(k0 + r * rb, rb), :] - cand_b[jb]
                    acc = acc + lax.shift_right_logical(pltpu.bitcast(diff, jnp.int32), 31)
                out.append(acc)
            return tuple(out)

        accs = lax.fori_loop(0, nkc, body, tuple(jnp.zeros((rb, LANES), jnp.int32) for _ in range(n_blk)))
        return jnp.concatenate([jnp.sum(a.astype(F32), axis=0, keepdims=True) for a in accs], axis=0)

    def ordinal_to_float(c):
        bits = jnp.where(c > 0, c, jnp.int32(INT_MIN) - c)
        return pltpu.bitcast(bits, F32)

    ncols = (nkc * kc_w).astype(F32)
    cnt0 = ncols - count_below(jnp.full((n_blk, LANES), -0.0, F32))
    nonneg = cnt0 >= kf
    prefix0 = jnp.where(nonneg, 0, jnp.int32(INT_MIN))
    cnt_at0 = jnp.where(nonneg, cnt0, ncols)

    def bit_body(b, carry):
        prefix, cnt_at = carry
        cand = prefix | lax.shift_left(jnp.int32(1), 30 - b)
        cnt = ncols - count_below(ordinal_to_float(cand))
        ok = cnt >= kf
        return jnp.where(ok, cand, prefix), jnp.where(ok, cnt, cnt_at)

    prefix, cnt_at = lax.fori_loop(0, 31, bit_body, (prefix0, cnt_at0))
    thr = jnp.where(prefix == jnp.int32(INT_MIN), -jnp.inf, ordinal_to_float(prefix))

    def count(pred):
        res = []
        for jb in range(n_blk):
            def body(kb, acc, jb=jb):
                k0 = pl.multiple_of(kb * rb, rb)
                kidx = k0 + lax.broadcasted_iota(jnp.int32, (rb, 1), 0)
                return acc + jnp.where(pred(sc_ref[jb, pl.ds(k0, rb), :], kidx, jb), 1.0, 0.0)
            acc = lax.fori_loop(0, nkc * (kc_w // rb), body, jnp.zeros((rb, LANES), F32))
            res.append(jnp.sum(acc, axis=0, keepdims=True))
        return jnp.concatenate(res, axis=0)

    tied = jnp.logical_and(cnt_at > kf, thr > NEG)
    any_tied = jnp.max(jnp.where(tied, 1.0, 0.0)) > 0.5

    @pl.when(any_tied)
    def _():
        row = lambda v, jb: v[jb:jb + 1, :]
        want = kf - count(lambda ch, kidx, jb: ch > row(thr, jb))

        def jbit(b, last):
            cand = last | lax.shift_left(jnp.int32(1), idx_bits - 1 - b)
            before = count(lambda ch, kidx, jb: jnp.logical_and(kidx < row(cand, jb), ch == row(thr, jb)))
            return jnp.where(before < want, cand, last)

        last = lax.fori_loop(0, idx_bits, jbit, jnp.zeros((n_blk, LANES), jnp.int32))

        for jb in range(n_blk):
            def drop(kb, carry, jb=jb):
                k0 = pl.multiple_of(kb * rb, rb)
                kidx = k0 + lax.broadcasted_iota(jnp.int32, (rb, 1), 0)
                ch = sc_ref[jb, pl.ds(k0, rb), :]
                dropped = jnp.logical_and(kidx > row(last, jb), ch == row(thr, jb))
                sc_ref[jb, pl.ds(k0, rb), :] = jnp.where(dropped, NEG, ch)
                return carry

            lax.fori_loop(0, nkc * (kc_w // rb), drop, 0)

    thr = jnp.maximum(thr, NEG_UP)

    m_ref[...] = jnp.full(m_ref.shape, NEG, F32)
    l_ref[...] = jnp.zeros(l_ref.shape, F32)
    acc_ref[...] = jnp.zeros(acc_ref.shape, F32)
    stack_heads(qs_ref, q_ref, N_ATT_HEADS)

    def logits(kc, s_ref):
        k0 = chunk_start(kc)
        bias = []
        for t in range(n_lg):
            blocks = [jnp.where(sc_ref[jb, pl.ds(k0 + t * LANES, LANES), :] >= thr[jb:jb + 1, :], 0.0, NEG).T
                      for jb in range(n_blk)]
            bias.append(jnp.concatenate(blocks, axis=0))
        d = lax.dot_general(qs_ref[...], kd_ref[pl.ds(k0, kc_w), :], _NT, preferred_element_type=F32)
        for h in range(N_ATT_HEADS):
            for t in range(n_lg):
                s_ref[h * qb:(h + 1) * qb, t * LANES:(t + 1) * LANES] = (
                    d[h * qb:(h + 1) * qb, t * LANES:(t + 1) * LANES] + bias[t])

    def softmax_pv(kc, s_ref):
        k0 = chunk_start(kc)
        alphas = []
        for h in range(N_ATT_HEADS):
            rows = slice(h * qb, (h + 1) * qb)
            masked = lambda t: s_ref[rows, t * LANES:(t + 1) * LANES]
            mx = masked(0)
            for t in range(1, n_lg):
                mx = jnp.maximum(mx, masked(t))
            m_old = m_ref[h]
            m_new = jnp.maximum(m_old, jnp.max(mx, axis=-1, keepdims=True))
            alpha = jnp.exp2(m_old - m_new)
            m_ref[h] = m_new
            psum = None
            for t in range(n_lg):
                p = jnp.exp2(masked(t) - m_new)
                p_ref[rows, t * LANES:(t + 1) * LANES] = p.astype(BF16)
                psum = p if psum is None else psum + p
            l_ref[h] = alpha * l_ref[h] + psum
            alphas.append(alpha)
        pv = _dot(p_ref[...], vd_ref[pl.ds(k0, kc_w), :])
        for g in range(N_ATT_HEADS // 2):
            pair = jnp.where(lo, pv[2 * g * qb:(2 * g + 1) * qb], pv[(2 * g + 1) * qb:(2 * g + 2) * qb])
            acc_ref[g] = acc_ref[g] * jnp.where(lo, alphas[2 * g], alphas[2 * g + 1]) + pair

    odd = lax.rem(nkc, 2)

    @pl.when(odd == 1)
    def _():
        logits(0, sa_ref)
        softmax_pv(0, sa_ref)

    @pl.when(nkc >= 2)
    def _():
        logits(odd, sa_ref)

    def att_pair(j, carry):
        kc = odd + 2 * j
        logits(kc + 1, sb_ref)
        softmax_pv(kc, sa_ref)
        logits(jnp.minimum(kc + 2, nkc - 1), sa_ref)
        softmax_pv(kc + 1, sb_ref)
        return carry

    lax.fori_loop(0, lax.div(nkc, 2), att_pair, 0)

    outs = []
    for g in range(N_ATT_HEADS // 2):
        l_lo = jnp.sum(l_ref[2 * g], axis=-1, keepdims=True)
        l_hi = jnp.sum(l_ref[2 * g + 1], axis=-1, keepdims=True)
        outs.append(acc_ref[g] * jnp.where(lo, 1.0 / l_lo, 1.0 / l_hi))
    o_ref[...] = (jnp.concatenate(outs, axis=1) * _silu(gatt_ref[...])).astype(BF16)


def _attention(q, qi, misc, gatt, kd, kid, vd, bsz, seq_len):
    m = q.shape[0]
    qb = min(256, seq_len)
    kc_w = min(512, seq_len)
    topk = min(INDEX_TOPK_MAX, seq_len // 4)
    assert seq_len % kc_w == 0 and kc_w % qb == 0 and qb % LANES == 0 and kc_w >= topk
    nqb = seq_len // qb
    qrow = lambda b, i: (b * nqb + i, 0)
    kv = lambda b, i: (b, 0)
    kern = functools.partial(_attn_kernel, qb=qb, kc_w=kc_w, topk=topk,
                             idx_bits=max(1, int(math.ceil(math.log2(seq_len)))))
    return pl.pallas_call(
        kern,
        grid=(bsz, nqb),
        in_specs=[pl.BlockSpec((qb, ATT_W), qrow),
                  pl.BlockSpec((qb, N_IDX_HEADS * IDX_DIM), qrow),
                  pl.BlockSpec((qb, MISC_W), qrow),
                  pl.BlockSpec((qb, ATT_W), qrow),
                  pl.BlockSpec((seq_len, LANES), kv),
                  pl.BlockSpec((seq_len, LANES), kv),
                  pl.BlockSpec((seq_len, LANES), kv)],
        out_specs=pl.BlockSpec((qb, ATT_W), qrow),
        out_shape=jax.ShapeDtypeStruct((m, ATT_W), BF16),
        scratch_shapes=[pltpu.VMEM((qb // LANES, seq_len, LANES), F32),
                        pltpu.VMEM((N_ATT_HEADS * qb, LANES), BF16),
                        pltpu.VMEM((N_IDX_HEADS * qb, LANES), BF16),
                        pltpu.VMEM((N_ATT_HEADS * qb, kc_w), F32),
                        pltpu.VMEM((N_ATT_HEADS * qb, kc_w), F32),
                        pltpu.VMEM((N_ATT_HEADS * qb, kc_w), BF16),
                        pltpu.VMEM((N_ATT_HEADS, qb, LANES), F32),
                        pltpu.VMEM((N_ATT_HEADS, qb, LANES), F32),
                        pltpu.VMEM((N_ATT_HEADS // 2, qb, LANES), F32)],
        compiler_params=pltpu.CompilerParams(vmem_limit_bytes=VMEM_LIMIT,
                                             dimension_semantics=("arbitrary", "arbitrary")),
        name="dsa_attention",
    )(q, qi, misc, gatt, kd, kid, vd)


def _ssd_kernel(xbc_ref, misc_ref, z_ref, cw_ref, cb_ref, dtb_ref, alog_ref, dsk_ref, ng_ref, e_ref,
                y_ref, ext_ref, st_ref):
    c = pl.program_id(1)
    gw = D_SSM // N_GROUPS
    tail = 8

    @pl.when(c == 0)
    def _():
        ext_ref[0:tail, :] = jnp.zeros((tail, CONV_CH), F32)
        st_ref[...] = jnp.zeros(st_ref.shape, F32)

    ext_ref[tail:tail + CHUNK, :] = xbc_ref[...]
    conv = cb_ref[...]
    for j in range(CONV_K):
        conv = conv + ext_ref[tail - j:tail - j + CHUNK, :] * cw_ref[CONV_K - 1 - j:CONV_K - j, :]
    ext_ref[0:tail, :] = ext_ref[CHUNK:CHUNK + tail, :]
    u = _silu(conv)
    xs = u[:, 0:D_SSM]
    bm = u[:, D_SSM:D_SSM + N_GROUPS * D_STATE]
    cm = u[:, D_SSM + N_GROUPS * D_STATE:]

    row = lax.broadcasted_iota(jnp.int32, (CHUNK, LANES), 0)
    lane = lax.broadcasted_iota(jnp.int32, (CHUNK, LANES), 1)
    tril = row >= lane
    lo = lane < SSM_HEAD_DIM

    dtr = misc_ref[...] + dtb_ref[...]
    softplus = jnp.maximum(dtr, 0.0) + jnp.log1p(jnp.exp(-jnp.abs(dtr)))
    dtv = jnp.where(lane < N_SSM_HEADS, softplus, 0.0)
    adt = dtv * (-jnp.exp(alog_ref[...]))
    acs = _dot_exact(jnp.where(tril, 1.0, 0.0), adt)
    acs_t = acs.T
    expand = e_ref[...]
    dt_e = _dot_exact(dtv, expand)
    acs_e = _dot_exact(acs, expand)
    last_e = acs_e[CHUNK - 1:CHUNK, :]
    xd = xs * dt_e
    w_state = (xd * jnp.exp(last_e - acs_e)).astype(BF16)
    decay_in = jnp.exp(acs_e)
    chunk_decay = jnp.exp(last_e)

    ys = []
    for g in range(N_GROUPS):
        bm_g = bm[:, g * D_STATE:(g + 1) * D_STATE]
        cmb = cm[:, g * D_STATE:(g + 1) * D_STATE].astype(BF16)
        cb = lax.dot_general(cmb, bm_g.astype(BF16), _NT, preferred_element_type=F32)
        st = st_ref[g]
        y_off = _dot(cmb, st.astype(BF16)) * decay_in[:, g * gw:(g + 1) * gw]
        st_ref[g] = st * chunk_decay[:, g * gw:(g + 1) * gw] + _dot(bm_g.T.astype(BF16), w_state[:, g * gw:(g + 1) * gw])
        for p in range(gw // LANES):
            pair = g * (gw // LANES) + p
            xp = xd[:, pair * LANES:(pair + 1) * LANES]
            acc = y_off[:, p * LANES:(p + 1) * LANES]
            for half in range(2):
                h = 2 * pair + half
                seg = acs[:, h:h + 1] - acs_t[h:h + 1, :]
                decay = jnp.exp(jnp.where(tril, seg, -jnp.inf))
                xh = jnp.where(lo if half == 0 else jnp.logical_not(lo), xp, 0.0).astype(BF16)
                acc = acc + _dot((cb * decay).astype(BF16), xh)
            ys.append(acc)
    y = jnp.concatenate(ys, axis=1) + dsk_ref[...] * xs
    y = y * _silu(z_ref[...])
    normed = []
    for g in range(N_GROUPS):
        yg = y[:, g * gw:(g + 1) * gw]
        normed.append(yg * lax.rsqrt(jnp.mean(yg * yg, axis=-1, keepdims=True) + EPS))
    y_ref[...] = (jnp.concatenate(normed, axis=1) * ng_ref[...]).astype(BF16)


def _ssd(xbc, misc, z, conv_w, conv_b, dt_bias, a_log, d_skip, ssm_norm_g, expand, bsz, seq_len):
    m = xbc.shape[0]
    nc = seq_len // CHUNK
    rowc = lambda b, c: (b * nc + c, 0)
    const = lambda b, c: (0, 0)
    pad = lambda v: jnp.pad(v, (0, LANES - v.shape[0])).reshape(1, LANES)
    return pl.pallas_call(
        _ssd_kernel,
        grid=(bsz, nc),
        in_specs=[pl.BlockSpec((CHUNK, CONV_CH), rowc),
                  pl.BlockSpec((CHUNK, MISC_W), rowc),
                  pl.BlockSpec((CHUNK, D_SSM), rowc),
                  pl.BlockSpec((CONV_K, CONV_CH), const),
                  pl.BlockSpec((1, CONV_CH), const),
                  pl.BlockSpec((1, LANES), const),
                  pl.BlockSpec((1, LANES), const),
                  pl.BlockSpec((1, D_SSM), const),
                  pl.BlockSpec((1, D_SSM), const),
                  pl.BlockSpec((LANES, D_SSM), const)],
        out_specs=pl.BlockSpec((CHUNK, D_SSM), rowc),
        out_shape=jax.ShapeDtypeStruct((m, D_SSM), BF16),
        scratch_shapes=[pltpu.VMEM((CHUNK + 8, CONV_CH), F32),
                        pltpu.VMEM((N_GROUPS, D_STATE, D_SSM // N_GROUPS), F32)],
        compiler_params=pltpu.CompilerParams(vmem_limit_bytes=VMEM_LIMIT,
                                             dimension_semantics=("arbitrary", "arbitrary")),
        name="ssd_mixer",
    )(xbc, misc, z, conv_w, conv_b.reshape(1, CONV_CH), pad(dt_bias), pad(a_log),
      jnp.repeat(d_skip, SSM_HEAD_DIM).reshape(1, D_SSM), ssm_norm_g.reshape(1, D_SSM), expand)


def _merge_kernel(x_ref, oa_ref, y_ref, gla_ref, gls_ref, gate_ref, wa_ref, ws_ref, wo_ref, fg_ref, o_ref,
                  *, final_norm):
    y_att = _dot(oa_ref[...], wa_ref[...])
    y_ssd = _dot(y_ref[...], ws_ref[...])
    merged = jax.nn.sigmoid(gla_ref[...]) * y_att + jax.nn.sigmoid(gls_ref[...]) * y_ssd
    out = x_ref[...] + gate_ref[0] * _dot(merged.astype(BF16), wo_ref[...])
    if final_norm:
        out = out * lax.rsqrt(jnp.mean(out * out, axis=-1, keepdims=True) + EPS) * fg_ref[...]
    o_ref[...] = out


def _merge(x2, oa, y, gla, gls, gate, w_a, w_s, w_o, final_g, seq_len, final_norm):
    m, d = x2.shape
    tm = 512
    nbl = seq_len // tm
    row = lambda i: (i, 0)
    const = lambda i: (0, 0)
    return pl.pallas_call(
        functools.partial(_merge_kernel, final_norm=final_norm),
        grid=(m // tm,),
        in_specs=[pl.BlockSpec((tm, d), row),
                  pl.BlockSpec((tm, ATT_W), row),
                  pl.BlockSpec((tm, D_SSM), row),
                  pl.BlockSpec((tm, d), row),
                  pl.BlockSpec((tm, d), row),
                  pl.BlockSpec((1, 1, d), lambda i: (i // nbl, 0, 0)),
                  pl.BlockSpec((ATT_W, d), const),
                  pl.BlockSpec((D_SSM, d), const),
                  pl.BlockSpec((d, d), const),
                  pl.BlockSpec((1, d), const)],
        out_specs=pl.BlockSpec((tm, d), row),
        out_shape=jax.ShapeDtypeStruct((m, d), F32),
        compiler_params=pltpu.CompilerParams(vmem_limit_bytes=VMEM_LIMIT),
        name="merge_out_projection",
    )(x2, oa, y, gla, gls, gate, w_a, w_s, w_o, final_g)


def _relayout_w_in(w_in):
    offs = np.cumsum((0,) + IN_SIZES)
    seg = [w_in[..., int(offs[i]):int(offs[i + 1])] for i in range(len(IN_SIZES))]
    q, k, v, g_att, qi, ki, wi, z, xbc, dt, gla, gls = seg
    zeros = lambda n: jnp.zeros(w_in.shape[:-1] + (n,), w_in.dtype)
    cols = [q, k, k, ki, ki, qi,
            v, v,
            g_att, dt, wi, zeros(MISC_W - N_SSM_HEADS - N_IDX_HEADS), z, xbc, gla, gls]
    return jnp.concatenate(cols, axis=-1).astype(BF16)


def _rope_tables(seq_len):
    inv = ROPE_THETA ** (-jnp.arange(0, HEAD_DIM, 2, dtype=F32) / HEAD_DIM)
    ang = jnp.arange(seq_len, dtype=F32)[:, None] * inv[None, :]
    cos, sin = jnp.cos(ang), jnp.sin(ang)
    return (jnp.concatenate([cos, cos, cos, cos], axis=-1),
            jnp.concatenate([-sin, sin, -sin, sin], axis=-1))


def kernel(x, c, w_ada, b_ada, norm_g, w_in, conv_w, conv_b, dt_bias, a_log, d_skip, ssm_norm_g,
           w_branch_a, w_branch_s, w_out, final_g):
    bsz, seq_len, d = x.shape
    depth = w_in.shape[0]
    assert d == D_MODEL and seq_len % CHUNK == 0
    mod = _modulation(c, w_ada, b_ada)
    w_r = _relayout_w_in(w_in)
    w_a = w_branch_a.astype(BF16)
    w_s = w_branch_s.astype(BF16)
    w_o = w_out.astype(BF16)
    cos_t, sin_t = _rope_tables(seq_len)
    expand = (jnp.arange(LANES)[:, None] == (jnp.arange(D_SSM) // SSM_HEAD_DIM)[None, :]).astype(F32)
    fg = final_g.reshape(1, d)

    x2 = x.reshape(bsz * seq_len, d)
    for i in range(depth):
        shift = mod[i, :, 0:d].reshape(bsz, 1, d)
        scale = mod[i, :, d:2 * d].reshape(bsz, 1, d)
        gate = mod[i, :, 2 * d:].reshape(bsz, 1, d)
        (q, kd, kid, qi, vd, gatt, misc, z, xbc, gla, gls) = _in_projection(
            x2, scale, shift, norm_g[i].reshape(1, d), w_r[i], cos_t, sin_t, seq_len)
        oa = _attention(q, qi, misc, gatt, kd, kid, vd, bsz, seq_len)
        y = _ssd(xbc, misc, z, conv_w[i], conv_b[i], dt_bias[i], a_log[i], d_skip[i], ssm_norm_g[i],
                 expand, bsz, seq_len)
        x2 = _merge(x2, oa, y, gla, gls, gate, w_a[i], w_s[i], w_o[i], fg, seq_len, i == depth - 1)
    return x2.reshape(bsz, seq_len, d)
```

```python
import functools
import math

import numpy as np
import jax
import jax.numpy as jnp
from jax import lax
from jax.experimental import pallas as pl
from jax.experimental.pallas import tpu as pltpu

D_MODEL = 1024
N_ATT_HEADS = 8
HEAD_DIM = 64
ATT_W = N_ATT_HEADS * HEAD_DIM
N_IDX_HEADS = 4
IDX_DIM = HEAD_DIM
INDEX_TOPK_MAX = 256
D_SSM = D_MODEL
SSM_HEAD_DIM = 64
N_SSM_HEADS = D_SSM // SSM_HEAD_DIM
N_GROUPS = 2
D_STATE = 128
CONV_K = 4
CHUNK = 128
CONV_CH = D_SSM + 2 * N_GROUPS * D_STATE
ROPE_THETA = 10000.0
EPS = 1e-6
NEG = -1e30
IN_SIZES = (ATT_W, HEAD_DIM, HEAD_DIM, ATT_W, N_IDX_HEADS * IDX_DIM, IDX_DIM, N_IDX_HEADS,
            D_SSM, CONV_CH, N_SSM_HEADS, D_MODEL, D_MODEL)

LANES = 128
ROPE_W = 1024
MISC_W = LANES
WI_OFF = N_SSM_HEADS
LOG2E = math.log2(math.e)
N_PROJ = ROPE_W + LANES + ATT_W + MISC_W + D_SSM + CONV_CH + 2 * D_MODEL
VMEM_LIMIT = 56 * 1024 * 1024
INT_MIN = -2 ** 31
NEG_UP = float(np.nextafter(np.float32(NEG), np.float32(0.0)))

F32 = jnp.float32
BF16 = jnp.bfloat16
_NT = (((1,), (1,)), ((), ()))


def _silu(v):
    return v * jax.nn.sigmoid(v)


def _dot(a, b):
    return jnp.dot(a, b, preferred_element_type=F32)


def _dot_exact(a, b):
    return jnp.dot(a, b, preferred_element_type=F32, precision=lax.Precision.HIGHEST)


def _mod_kernel(c_ref, w_ref, b_ref, o_ref):
    o_ref[0] = _dot_exact(_silu(c_ref[...]), w_ref[0]) + b_ref[0]


def _modulation(c, w_ada, b_ada):
    depth, d, n = w_ada.shape
    bsz = c.shape[0]
    tn = 1024
    return pl.pallas_call(
        _mod_kernel,
        grid=(depth, n // tn),
        in_specs=[pl.BlockSpec((bsz, d), lambda i, j: (0, 0)),
                  pl.BlockSpec((1, d, tn), lambda i, j: (i, 0, j)),
                  pl.BlockSpec((1, 1, tn), lambda i, j: (i, 0, j))],
        out_specs=pl.BlockSpec((1, bsz, tn), lambda i, j: (i, 0, j)),
        out_shape=jax.ShapeDtypeStruct((depth, bsz, n), F32),
        compiler_params=pltpu.CompilerParams(vmem_limit_bytes=VMEM_LIMIT),
        name="adaln_modulation",
    )(c, w_ada, b_ada.reshape(depth, 1, n))


def _inproj_kernel(x_ref, scale_ref, shift_ref, g_ref, w_ref, cos_ref, sin_ref,
                   q_ref, kd_ref, kid_ref, qi_ref, vd_ref, gatt_ref, misc_ref,
                   z_ref, xbc_ref, gla_ref, gls_ref):
    x = x_ref[...]
    h = x * lax.rsqrt(jnp.mean(x * x, axis=-1, keepdims=True) + EPS) * g_ref[...]
    h = h * (1.0 + scale_ref[0]) + shift_ref[0]
    hb = h.astype(BF16)

    def proj(a, b):
        return _dot(hb, w_ref[:, a:b])

    r = proj(0, ROPE_W)
    cos = cos_ref[...]
    sin = sin_ref[...]
    lane = lax.broadcasted_iota(jnp.int32, cos.shape, 1)
    first_half = (lane % HEAD_DIM) < (HEAD_DIM // 2)
    roped = []
    for gi in range(ROPE_W // LANES):
        u = r[:, gi * LANES:(gi + 1) * LANES]
        rot = jnp.where(first_half, pltpu.roll(u, LANES - HEAD_DIM // 2, 1), pltpu.roll(u, HEAD_DIM // 2, 1))
        roped.append(u * cos + rot * sin)
    q_ref[...] = (jnp.concatenate(roped[0:4], axis=1) * (HEAD_DIM ** -0.5 * LOG2E)).astype(BF16)
    kd_ref[...] = roped[4].astype(BF16)
    kid_ref[...] = roped[5].astype(BF16)
    qi_ref[...] = (jnp.concatenate(roped[6:8], axis=1) * IDX_DIM ** -0.5).astype(BF16)

    o = ROPE_W
    vd_ref[...] = proj(o, o + LANES).astype(BF16)
    o += LANES
    gatt_ref[...] = proj(o, o + ATT_W)
    o += ATT_W
    misc_ref[...] = proj(o, o + MISC_W)
    o += MISC_W
    z_ref[...] = proj(o, o + D_SSM)
    o += D_SSM
    xbc_ref[...] = proj(o, o + CONV_CH)
    o += CONV_CH
    gla_ref[...] = proj(o, o + D_MODEL)
    o += D_MODEL
    gls_ref[...] = proj(o, o + D_MODEL)


def _in_projection(x2, scale, shift, norm_g, w_r, cos_t, sin_t, seq_len):
    m, d = x2.shape
    tm = 256
    nbl = seq_len // tm
    row = lambda i: (i, 0)
    widths = [(ATT_W, BF16), (LANES, BF16), (LANES, BF16), (N_IDX_HEADS * IDX_DIM, BF16),
              (LANES, BF16), (ATT_W, F32), (MISC_W, F32),
              (D_SSM, F32), (CONV_CH, F32), (D_MODEL, F32), (D_MODEL, F32)]
    return pl.pallas_call(
        _inproj_kernel,
        grid=(m // tm,),
        in_specs=[pl.BlockSpec((tm, d), row),
                  pl.BlockSpec((1, 1, d), lambda i: (i // nbl, 0, 0)),
                  pl.BlockSpec((1, 1, d), lambda i: (i // nbl, 0, 0)),
                  pl.BlockSpec((1, d), lambda i: (0, 0)),
                  pl.BlockSpec((d, N_PROJ), lambda i: (0, 0)),
                  pl.BlockSpec((tm, LANES), lambda i: (i % nbl, 0)),
                  pl.BlockSpec((tm, LANES), lambda i: (i % nbl, 0))],
        out_specs=[pl.BlockSpec((tm, w), row) for w, _ in widths],
        out_shape=[jax.ShapeDtypeStruct((m, w), dt) for w, dt in widths],
        compiler_params=pltpu.CompilerParams(vmem_limit_bytes=VMEM_LIMIT),
        name="in_projection",
    )(x2, scale, shift, norm_g, w_r, cos_t, sin_t)


def _attn_kernel(q_ref, qi_ref, misc_ref, gatt_ref, kd_ref, kid_ref, vd_ref, tri_ref, o_ref,
                 sc_ref, qs_ref, qis_ref, sa_ref, sb_ref, p_ref, m_ref, l_ref, acc_ref,
                 *, qb, kc_w, topk):
    i = pl.program_id(1)
    q0 = i * qb
    nkc = lax.div(q0 + qb + kc_w - 1, kc_w)
    lane = lax.broadcasted_iota(jnp.int32, (qb, LANES), 1)
    lo = lane < HEAD_DIM
    kf = float(topk)
    n_lg = kc_w // LANES
    rb = 64

    def stack_heads(dst_ref, src_ref, n_heads):
        for h in range(n_heads):
            grp = src_ref[:, (h // 2) * LANES:(h // 2 + 1) * LANES].astype(F32)
            keep = lo if h % 2 == 0 else jnp.logical_not(lo)
            dst_ref[h * qb:(h + 1) * qb, :] = jnp.where(keep, grp, 0.0).astype(BF16)

    def chunk_start(kc):
        return pl.multiple_of(kc * kc_w, kc_w)

    stack_heads(qis_ref, qi_ref, N_IDX_HEADS)
    n_blk = qb // LANES
    blk_lane = lax.broadcasted_iota(jnp.int32, (1, LANES), 1)
    w_rows = []
    for jb in range(n_blk):
        misc_t = misc_ref[jb * LANES:(jb + 1) * LANES, :].T
        w_rows.append([misc_t[WI_OFF + h:WI_OFF + h + 1, :] * N_IDX_HEADS ** -0.5 for h in range(N_IDX_HEADS)])

    def idx_body(kc, masked):
        k0 = chunk_start(kc)
        d = lax.dot_general(kid_ref[pl.ds(k0, kc_w), :], qis_ref[...], _NT, preferred_element_type=F32)
        for jb in range(n_blk):
            s = None
            for h in range(N_IDX_HEADS):
                c0 = h * qb + jb * LANES
                term = jnp.maximum(d[:, c0:c0 + LANES], 0.0) * w_rows[jb][h]
                s = term if s is None else s + term
            if masked:
                kpos = k0 + lax.broadcasted_iota(jnp.int32, (kc_w, 1), 0)
                s = jnp.where(kpos <= q0 + jb * LANES + blk_lane, s, NEG)
            sc_ref[jb, pl.ds(k0, kc_w), :] = s

    def idx_full(kc, carry):
        idx_body(kc, False)
        return carry

    lax.fori_loop(0, nkc - 1, idx_full, 0)
    idx_body(nkc - 1, True)

    def count_below(cand):
        cand_b = [jnp.broadcast_to(cand[jb:jb + 1, :], (rb, LANES)) for jb in range(n_blk)]

        def body(kc, accs):
            k0 = chunk_start(kc)
            out = []
            for jb in range(n_blk):
                acc = accs[jb]
                for r in range(kc_w // rb):
                    diff = sc_ref[jb, pl.ds(k0 + r * rb, rb), :] - cand_b[jb]
                    acc = acc + lax.shift_right_logical(pltpu.bitcast(diff, jnp.int32), 31)
                out.append(acc)
            return tuple(out)

        accs = lax.fori_loop(0, nkc, body, tuple(jnp.zeros((rb, LANES), jnp.int32) for _ in range(n_blk)))
        return jnp.concatenate([jnp.sum(a.astype(F32), axis=0, keepdims=True) for a in accs], axis=0)

    def ordinal_to_float(c):
        bits = jnp.where(c > 0, c, jnp.int32(INT_MIN) - c)
        return pltpu.bitcast(bits, F32)

    ncols = (nkc * kc_w).astype(F32)
    cnt0 = ncols - count_below(jnp.full((n_blk, LANES), -0.0, F32))
    nonneg = cnt0 >= kf
    prefix0 = jnp.where(nonneg, 0, jnp.int32(INT_MIN))
    cnt_at0 = jnp.where(nonneg, cnt0, ncols)

    def bit_body(b, carry):
        prefix, cnt_at = carry
        cand = prefix | lax.shift_left(jnp.int32(1), 30 - b)
        cnt = ncols - count_below(ordinal_to_float(cand))
        ok = cnt >= kf
        return jnp.where(ok, cand, prefix), jnp.where(ok, cnt, cnt_at)

    prefix, cnt_at = lax.fori_loop(0, 31, bit_body, (prefix0, cnt_at0))
    thr = jnp.where(prefix == jnp.int32(INT_MIN), -jnp.inf, ordinal_to_float(prefix))

    tied = jnp.logical_and(cnt_at > kf, thr > NEG)
    any_tied = jnp.max(jnp.where(tied, 1.0, 0.0)) > 0.5

    @pl.when(any_tied)
    def _():
        above = ncols - count_below(ordinal_to_float(prefix + 1))
        want = kf - above
        for jb in range(n_blk):
            def drop(kc, seen, jb=jb):
                k0 = chunk_start(kc)
                ch = sc_ref[jb, pl.ds(k0, kc_w), :]
                tie = ch == thr[jb:jb + 1, :]
                rank = seen + _dot(tri_ref[...], jnp.where(tie, 1.0, 0.0).astype(BF16))
                dropped = jnp.logical_and(tie, rank > want[jb:jb + 1, :])
                sc_ref[jb, pl.ds(k0, kc_w), :] = jnp.where(dropped, NEG, ch)
                return rank[kc_w - 1:kc_w, :]

            lax.fori_loop(0, nkc, drop, jnp.zeros((1, LANES), F32))

    thr = jnp.maximum(thr, NEG_UP)

    m_ref[...] = jnp.full(m_ref.shape, NEG, F32)
    l_ref[...] = jnp.zeros(l_ref.shape, F32)
    acc_ref[...] = jnp.zeros(acc_ref.shape, F32)
    stack_heads(qs_ref, q_ref, N_ATT_HEADS)

    def logits(kc, s_ref):
        k0 = chunk_start(kc)
        bias = []
        for t in range(n_lg):
            blocks = [jnp.where(sc_ref[jb, pl.ds(k0 + t * LANES, LANES), :] >= thr[jb:jb + 1, :], 0.0, NEG).T
                      for jb in range(n_blk)]
            bias.append(jnp.concatenate(blocks, axis=0))
        d = lax.dot_general(qs_ref[...], kd_ref[pl.ds(k0, kc_w), :], _NT, preferred_element_type=F32)
        for h in range(N_ATT_HEADS):
            for t in range(n_lg):
                s_ref[h * qb:(h + 1) * qb, t * LANES:(t + 1) * LANES] = (
                    d[h * qb:(h + 1) * qb, t * LANES:(t + 1) * LANES] + bias[t])

    def softmax_pv(kc, s_ref):
        k0 = chunk_start(kc)
        alphas = []
        for h in range(N_ATT_HEADS):
            rows = slice(h * qb, (h + 1) * qb)
            masked = lambda t: s_ref[rows, t * LANES:(t + 1) * LANES]
            mx = masked(0)
            for t in range(1, n_lg):
                mx = jnp.maximum(mx, masked(t))
            m_old = m_ref[h]
            m_new = jnp.maximum(m_old, jnp.max(mx, axis=-1, keepdims=True))
            alpha = jnp.exp2(m_old - m_new)
            m_ref[h] = m_new
            psum = None
            for t in range(n_lg):
                p = jnp.exp2(masked(t) - m_new)
                p_ref[rows, t * LANES:(t + 1) * LANES] = p.astype(BF16)
                psum = p if psum is None else psum + p
            l_ref[h] = alpha * l_ref[h] + psum
            alphas.append(alpha)
        pv = _dot(p_ref[...], vd_ref[pl.ds(k0, kc_w), :])
        for g in range(N_ATT_HEADS // 2):
            pair = jnp.where(lo, pv[2 * g * qb:(2 * g + 1) * qb], pv[(2 * g + 1) * qb:(2 * g + 2) * qb])
            acc_ref[g] = acc_ref[g] * jnp.where(lo, alphas[2 * g], alphas[2 * g + 1]) + pair

    odd = lax.rem(nkc, 2)

    @pl.when(odd == 1)
    def _():
        logits(0, sa_ref)
        softmax_pv(0, sa_ref)

    @pl.when(nkc >= 2)
    def _():
        logits(odd, sa_ref)

    def att_pair(j, carry):
        kc = odd + 2 * j
        logits(kc + 1, sb_ref)
        softmax_pv(kc, sa_ref)
        logits(jnp.minimum(kc + 2, nkc - 1), sa_ref)
        softmax_pv(kc + 1, sb_ref)
        return carry

    lax.fori_loop(0, lax.div(nkc, 2), att_pair, 0)

    outs = []
    for g in range(N_ATT_HEADS // 2):
        l_lo = jnp.sum(l_ref[2 * g], axis=-1, keepdims=True)
        l_hi = jnp.sum(l_ref[2 * g + 1], axis=-1, keepdims=True)
        outs.append(acc_ref[g] * jnp.where(lo, 1.0 / l_lo, 1.0 / l_hi))
    o_ref[...] = (jnp.concatenate(outs, axis=1) * _silu(gatt_ref[...])).astype(BF16)


def _attention(q, qi, misc, gatt, kd, kid, vd, bsz, seq_len):
    m = q.shape[0]
    qb = min(256, seq_len)
    kc_w = min(512, seq_len)
    topk = min(INDEX_TOPK_MAX, seq_len // 4)
    assert seq_len % kc_w == 0 and kc_w % qb == 0 and qb % LANES == 0 and kc_w >= topk
    nqb = seq_len // qb
    qrow = lambda b, i: (b * nqb + i, 0)
    kv = lambda b, i: (b, 0)
    kern = functools.partial(_attn_kernel, qb=qb, kc_w=kc_w, topk=topk)
    tri = (jnp.arange(kc_w)[:, None] >= jnp.arange(kc_w)[None, :]).astype(BF16)
    return pl.pallas_call(
        kern,
        grid=(bsz, nqb),
        in_specs=[pl.BlockSpec((qb, ATT_W), qrow),
                  pl.BlockSpec((qb, N_IDX_HEADS * IDX_DIM), qrow),
                  pl.BlockSpec((qb, MISC_W), qrow),
                  pl.BlockSpec((qb, ATT_W), qrow),
                  pl.BlockSpec((seq_len, LANES), kv),
                  pl.BlockSpec((seq_len, LANES), kv),
                  pl.BlockSpec((seq_len, LANES), kv),
                  pl.BlockSpec((kc_w, kc_w), lambda b, i: (0, 0))],
        out_specs=pl.BlockSpec((qb, ATT_W), qrow),
        out_shape=jax.ShapeDtypeStruct((m, ATT_W), BF16),
        scratch_shapes=[pltpu.VMEM((qb // LANES, seq_len, LANES), F32),
                        pltpu.VMEM((N_ATT_HEADS * qb, LANES), BF16),
                        pltpu.VMEM((N_IDX_HEADS * qb, LANES), BF16),
                        pltpu.VMEM((N_ATT_HEADS * qb, kc_w), F32),
                        pltpu.VMEM((N_ATT_HEADS * qb, kc_w), F32),
                        pltpu.VMEM((N_ATT_HEADS * qb, kc_w), BF16),
                        pltpu.VMEM((N_ATT_HEADS, qb, LANES), F32),
                        pltpu.VMEM((N_ATT_HEADS, qb, LANES), F32),
                        pltpu.VMEM((N_ATT_HEADS // 2, qb, LANES), F32)],
        compiler_params=pltpu.CompilerParams(vmem_limit_bytes=VMEM_LIMIT,
                                             dimension_semantics=("arbitrary", "arbitrary")),
        name="dsa_attention",
    )(q, qi, misc, gatt, kd, kid, vd, tri)


def _ssd_kernel(xbc_ref, misc_ref, z_ref, cw_ref, cb_ref, dtb_ref, alog_ref, dsk_ref, ng_ref, e_ref,
                y_ref, ext_ref, st_ref):
    c = pl.program_id(1)
    gw = D_SSM // N_GROUPS
    tail = 8

    @pl.when(c == 0)
    def _():
        ext_ref[0:tail, :] = jnp.zeros((tail, CONV_CH), F32)
        st_ref[...] = jnp.zeros(st_ref.shape, F32)

    ext_ref[tail:tail + CHUNK, :] = xbc_ref[...]
    conv = cb_ref[...]
    for j in range(CONV_K):
        conv = conv + ext_ref[tail - j:tail - j + CHUNK, :] * cw_ref[CONV_K - 1 - j:CONV_K - j, :]
    ext_ref[0:tail, :] = ext_ref[CHUNK:CHUNK + tail, :]
    u = _silu(conv)
    xs = u[:, 0:D_SSM]
    bm = u[:, D_SSM:D_SSM + N_GROUPS * D_STATE]
    cm = u[:, D_SSM + N_GROUPS * D_STATE:]

    row = lax.broadcasted_iota(jnp.int32, (CHUNK, LANES), 0)
    lane = lax.broadcasted_iota(jnp.int32, (CHUNK, LANES), 1)
    tril = row >= lane
    lo = lane < SSM_HEAD_DIM

    dtr = misc_ref[...] + dtb_ref[...]
    softplus = jnp.maximum(dtr, 0.0) + jnp.log1p(jnp.exp(-jnp.abs(dtr)))
    dtv = jnp.where(lane < N_SSM_HEADS, softplus, 0.0)
    adt = dtv * (-jnp.exp(alog_ref[...]))
    acs = _dot_exact(jnp.where(tril, 1.0, 0.0), adt)
    acs_t = acs.T
    expand = e_ref[...]
    dt_e = _dot_exact(dtv, expand)
    acs_e = _dot_exact(acs, expand)
    last_e = acs_e[CHUNK - 1:CHUNK, :]
    xd = xs * dt_e
    w_state = (xd * jnp.exp(last_e - acs_e)).astype(BF16)
    decay_in = jnp.exp(acs_e)
    chunk_decay = jnp.exp(last_e)

    ys = []
    for g in range(N_GROUPS):
        bm_g = bm[:, g * D_STATE:(g + 1) * D_STATE]
        cmb = cm[:, g * D_STATE:(g + 1) * D_STATE].astype(BF16)
        cb = lax.dot_general(cmb, bm_g.astype(BF16), _NT, preferred_element_type=F32)
        st = st_ref[g]
        y_off = _dot(cmb, st.astype(BF16)) * decay_in[:, g * gw:(g + 1) * gw]
        st_ref[g] = st * chunk_decay[:, g * gw:(g + 1) * gw] + _dot(bm_g.T.astype(BF16), w_state[:, g * gw:(g + 1) * gw])
        for p in range(gw // LANES):
            pair = g * (gw // LANES) + p
            xp = xd[:, pair * LANES:(pair + 1) * LANES]
            acc = y_off[:, p * LANES:(p + 1) * LANES]
            for half in range(2):
                h = 2 * pair + half
                seg = acs[:, h:h + 1] - acs_t[h:h + 1, :]
                decay = jnp.exp(jnp.where(tril, seg, -jnp.inf))
                xh = jnp.where(lo if half == 0 else jnp.logical_not(lo), xp, 0.0).astype(BF16)
                acc = acc + _dot((cb * decay).astype(BF16), xh)
            ys.append(acc)
    y = jnp.concatenate(ys, axis=1) + dsk_ref[...] * xs
    y = y * _silu(z_ref[...])
    normed = []
    for g in range(N_GROUPS):
        yg = y[:, g * gw:(g + 1) * gw]
        normed.append(yg * lax.rsqrt(jnp.mean(yg * yg, axis=-1, keepdims=True) + EPS))
    y_ref[...] = (jnp.concatenate(normed, axis=1) * ng_ref[...]).astype(BF16)


def _ssd(xbc, misc, z, conv_w, conv_b, dt_bias, a_log, d_skip, ssm_norm_g, expand, bsz, seq_len):
    m = xbc.shape[0]
    nc = seq_len // CHUNK
    rowc = lambda b, c: (b * nc + c, 0)
    const = lambda b, c: (0, 0)
    pad = lambda v: jnp.pad(v, (0, LANES - v.shape[0])).reshape(1, LANES)
    return pl.pallas_call(
        _ssd_kernel,
        grid=(bsz, nc),
        in_specs=[pl.BlockSpec((CHUNK, CONV_CH), rowc),
                  pl.BlockSpec((CHUNK, MISC_W), rowc),
                  pl.BlockSpec((CHUNK, D_SSM), rowc),
                  pl.BlockSpec((CONV_K, CONV_CH), const),
                  pl.BlockSpec((1, CONV_CH), const),
                  pl.BlockSpec((1, LANES), const),
                  pl.BlockSpec((1, LANES), const),
                  pl.BlockSpec((1, D_SSM), const),
                  pl.BlockSpec((1, D_SSM), const),
                  pl.BlockSpec((LANES, D_SSM), const)],
        out_specs=pl.BlockSpec((CHUNK, D_SSM), rowc),
        out_shape=jax.ShapeDtypeStruct((m, D_SSM), BF16),
        scratch_shapes=[pltpu.VMEM((CHUNK + 8, CONV_CH), F32),
                        pltpu.VMEM((N_GROUPS, D_STATE, D_SSM // N_GROUPS), F32)],
        compiler_params=pltpu.CompilerParams(vmem_limit_bytes=VMEM_LIMIT,
                                             dimension_semantics=("arbitrary", "arbitrary")),
        name="ssd_mixer",
    )(xbc, misc, z, conv_w, conv_b.reshape(1, CONV_CH), pad(dt_bias), pad(a_log),
      jnp.repeat(d_skip, SSM_HEAD_DIM).reshape(1, D_SSM), ssm_norm_g.reshape(1, D_SSM), expand)


def _merge_kernel(x_ref, oa_ref, y_ref, gla_ref, gls_ref, gate_ref, wa_ref, ws_ref, wo_ref, fg_ref, o_ref,
                  *, final_norm):
    y_att = _dot(oa_ref[...], wa_ref[...])
    y_ssd = _dot(y_ref[...], ws_ref[...])
    merged = jax.nn.sigmoid(gla_ref[...]) * y_att + jax.nn.sigmoid(gls_ref[...]) * y_ssd
    out = x_ref[...] + gate_ref[0] * _dot(merged.astype(BF16), wo_ref[...])
    if final_norm:
        out = out * lax.rsqrt(jnp.mean(out * out, axis=-1, keepdims=True) + EPS) * fg_ref[...]
    o_ref[...] = out


def _merge(x2, oa, y, gla, gls, gate, w_a, w_s, w_o, final_g, seq_len, final_norm):
    m, d = x2.shape
    tm = 512
    nbl = seq_len // tm
    row = lambda i: (i, 0)
    const = lambda i: (0, 0)
    return pl.pallas_call(
        functools.partial(_merge_kernel, final_norm=final_norm),
        grid=(m // tm,),
        in_specs=[pl.BlockSpec((tm, d), row),
                  pl.BlockSpec((tm, ATT_W), row),
                  pl.BlockSpec((tm, D_SSM), row),
                  pl.BlockSpec((tm, d), row),
                  pl.BlockSpec((tm, d), row),
                  pl.BlockSpec((1, 1, d), lambda i: (i // nbl, 0, 0)),
                  pl.BlockSpec((ATT_W, d), const),
                  pl.BlockSpec((D_SSM, d), const),
                  pl.BlockSpec((d, d), const),
                  pl.BlockSpec((1, d), const)],
        out_specs=pl.BlockSpec((tm, d), row),
        out_shape=jax.ShapeDtypeStruct((m, d), F32),
        compiler_params=pltpu.CompilerParams(vmem_limit_bytes=VMEM_LIMIT),
        name="merge_out_projection",
    )(x2, oa, y, gla, gls, gate, w_a, w_s, w_o, final_g)


def _relayout_w_in(w_in):
    offs = np.cumsum((0,) + IN_SIZES)
    seg = [w_in[..., int(offs[i]):int(offs[i + 1])] for i in range(len(IN_SIZES))]
    q, k, v, g_att, qi, ki, wi, z, xbc, dt, gla, gls = seg
    zeros = lambda n: jnp.zeros(w_in.shape[:-1] + (n,), w_in.dtype)
    cols = [q, k, k, ki, ki, qi,
            v, v,
            g_att, dt, wi, zeros(MISC_W - N_SSM_HEADS - N_IDX_HEADS), z, xbc, gla, gls]
    return jnp.concatenate(cols, axis=-1).astype(BF16)


def _rope_tables(seq_len):
    inv = ROPE_THETA ** (-jnp.arange(0, HEAD_DIM, 2, dtype=F32) / HEAD_DIM)
    ang = jnp.arange(seq_len, dtype=F32)[:, None] * inv[None, :]
    cos, sin = jnp.cos(ang), jnp.sin(ang)
    return (jnp.concatenate([cos, cos, cos, cos], axis=-1),
            jnp.concatenate([-sin, sin, -sin, sin], axis=-1))


def kernel(x, c, w_ada, b_ada, norm_g, w_in, conv_w, conv_b, dt_bias, a_log, d_skip, ssm_norm_g,
           w_branch_a, w_branch_s, w_out, final_g):
    bsz, seq_len, d = x.shape
    depth = w_in.shape[0]
    assert d == D_MODEL and seq_len % CHUNK == 0
    mod = _modulation(c, w_ada, b_ada)
    w_r = _relayout_w_in(w_in)
    w_a = w_branch_a.astype(BF16)
    w_s = w_branch_s.astype(BF16)
    w_o = w_out.astype(BF16)
    cos_t, sin_t = _rope_tables(seq_len)
    expand = (jnp.arange(LANES)[:, None] == (jnp.arange(D_SSM) // SSM_HEAD_DIM)[None, :]).astype(F32)
    fg = final_g.reshape(1, d)

    x2 = x.reshape(bsz * seq_len, d)
    for i in range(depth):
        shift = mod[i, :, 0:d].reshape(bsz, 1, d)
        scale = mod[i, :, d:2 * d].reshape(bsz, 1, d)
        gate = mod[i, :, 2 * d:].reshape(bsz, 1, d)
        (q, kd, kid, qi, vd, gatt, misc, z, xbc, gla, gls) = _in_projection(
            x2, scale, shift, norm_g[i].reshape(1, d), w_r[i], cos_t, sin_t, seq_len)
        oa = _attention(q, qi, misc, gatt, kd, kid, vd, bsz, seq_len)
        y = _ssd(xbc, misc, z, conv_w[i], conv_b[i], dt_bias[i], a_log[i], d_skip[i], ssm_norm_g[i],
                 expand, bsz, seq_len)
        x2 = _merge(x2, oa, y, gla, gls, gate, w_a[i], w_s[i], w_o[i], fg, seq_len, i == depth - 1)
    return x2.reshape(bsz, seq_len, d)
```

```python
import functools
import math

import numpy as np
import jax
import jax.numpy as jnp
from jax import lax
from jax.experimental import pallas as pl
from jax.experimental.pallas import tpu as pltpu

D_MODEL = 1024
N_ATT_HEADS = 8
HEAD_DIM = 64
ATT_W = N_ATT_HEADS * HEAD_DIM
N_IDX_HEADS = 4
IDX_DIM = HEAD_DIM
INDEX_TOPK_MAX = 256
D_SSM = D_MODEL
SSM_HEAD_DIM = 64
N_SSM_HEADS = D_SSM // SSM_HEAD_DIM
N_GROUPS = 2
D_STATE = 128
CONV_K = 4
CHUNK = 128
CONV_CH = D_SSM + 2 * N_GROUPS * D_STATE
ROPE_THETA = 10000.0
EPS = 1e-6
NEG = -1e30
IN_SIZES = (ATT_W, HEAD_DIM, HEAD_DIM, ATT_W, N_IDX_HEADS * IDX_DIM, IDX_DIM, N_IDX_HEADS,
            D_SSM, CONV_CH, N_SSM_HEADS, D_MODEL, D_MODEL)

LANES = 128
ROPE_W = 1024
MISC_W = LANES
WI_OFF = N_SSM_HEADS
LOG2E = math.log2(math.e)
N_PROJ = ROPE_W + LANES + ATT_W + MISC_W + D_SSM + CONV_CH + 2 * D_MODEL
VMEM_LIMIT = 56 * 1024 * 1024
INT_MIN = -2 ** 31
NEG_UP = float(np.nextafter(np.float32(NEG), np.float32(0.0)))

F32 = jnp.float32
BF16 = jnp.bfloat16
_NT = (((1,), (1,)), ((), ()))


def _silu(v):
    return v * jax.nn.sigmoid(v)


def _dot(a, b):
    return jnp.dot(a, b, preferred_element_type=F32)


def _dot_exact(a, b):
    return jnp.dot(a, b, preferred_element_type=F32, precision=lax.Precision.HIGHEST)


def _split3(x):
    hi = x.astype(BF16)
    rest = x - hi.astype(F32)
    mid = rest.astype(BF16)
    return hi, mid, (rest - mid.astype(F32)).astype(BF16)


def _dot_sel(a, b):
    if a.dtype == BF16:
        return sum(_dot(a, piece) for piece in _split3(b))
    return sum(_dot(piece, b) for piece in _split3(a))


def _mod_kernel(c_ref, w_ref, b_ref, o_ref):
    o_ref[0] = _dot_exact(_silu(c_ref[...]), w_ref[0]) + b_ref[0]


def _modulation(c, w_ada, b_ada):
    depth, d, n = w_ada.shape
    bsz = c.shape[0]
    tn = 1024
    return pl.pallas_call(
        _mod_kernel,
        grid=(depth, n // tn),
        in_specs=[pl.BlockSpec((bsz, d), lambda i, j: (0, 0)),
                  pl.BlockSpec((1, d, tn), lambda i, j: (i, 0, j)),
                  pl.BlockSpec((1, 1, tn), lambda i, j: (i, 0, j))],
        out_specs=pl.BlockSpec((1, bsz, tn), lambda i, j: (i, 0, j)),
        out_shape=jax.ShapeDtypeStruct((depth, bsz, n), F32),
        compiler_params=pltpu.CompilerParams(vmem_limit_bytes=VMEM_LIMIT),
        name="adaln_modulation",
    )(c, w_ada, b_ada.reshape(depth, 1, n))


def _inproj_kernel(x_ref, scale_ref, shift_ref, g_ref, w_ref, cos_ref, sin_ref,
                   q_ref, kd_ref, kid_ref, qi_ref, vd_ref, gatt_ref, misc_ref,
                   z_ref, xbc_ref, gla_ref, gls_ref):
    x = x_ref[...]
    h = x * lax.rsqrt(jnp.mean(x * x, axis=-1, keepdims=True) + EPS) * g_ref[...]
    h = h * (1.0 + scale_ref[0]) + shift_ref[0]
    hb = h.astype(BF16)

    def proj(a, b):
        return _dot(hb, w_ref[:, a:b])

    r = proj(0, ROPE_W)
    cos = cos_ref[...]
    sin = sin_ref[...]
    lane = lax.broadcasted_iota(jnp.int32, cos.shape, 1)
    first_half = (lane % HEAD_DIM) < (HEAD_DIM // 2)
    roped = []
    for gi in range(ROPE_W // LANES):
        u = r[:, gi * LANES:(gi + 1) * LANES]
        rot = jnp.where(first_half, pltpu.roll(u, LANES - HEAD_DIM // 2, 1), pltpu.roll(u, HEAD_DIM // 2, 1))
        roped.append(u * cos + rot * sin)
    q_ref[...] = (jnp.concatenate(roped[0:4], axis=1) * (HEAD_DIM ** -0.5 * LOG2E)).astype(BF16)
    kd_ref[...] = roped[4].astype(BF16)
    kid_ref[...] = roped[5].astype(BF16)
    qi_ref[...] = (jnp.concatenate(roped[6:8], axis=1) * IDX_DIM ** -0.5).astype(BF16)

    o = ROPE_W
    vd_ref[...] = proj(o, o + LANES).astype(BF16)
    o += LANES
    gatt_ref[...] = proj(o, o + ATT_W)
    o += ATT_W
    misc_ref[...] = proj(o, o + MISC_W)
    o += MISC_W
    z_ref[...] = proj(o, o + D_SSM)
    o += D_SSM
    xbc_ref[...] = proj(o, o + CONV_CH)
    o += CONV_CH
    gla_ref[...] = proj(o, o + D_MODEL)
    o += D_MODEL
    gls_ref[...] = proj(o, o + D_MODEL)


def _in_projection(x2, scale, shift, norm_g, w_r, cos_t, sin_t, seq_len):
    m, d = x2.shape
    tm = 256
    nbl = seq_len // tm
    row = lambda i: (i, 0)
    widths = [(ATT_W, BF16), (LANES, BF16), (LANES, BF16), (N_IDX_HEADS * IDX_DIM, BF16),
              (LANES, BF16), (ATT_W, F32), (MISC_W, F32),
              (D_SSM, F32), (CONV_CH, F32), (D_MODEL, F32), (D_MODEL, F32)]
    return pl.pallas_call(
        _inproj_kernel,
        grid=(m // tm,),
        in_specs=[pl.BlockSpec((tm, d), row),
                  pl.BlockSpec((1, 1, d), lambda i: (i // nbl, 0, 0)),
                  pl.BlockSpec((1, 1, d), lambda i: (i // nbl, 0, 0)),
                  pl.BlockSpec((1, d), lambda i: (0, 0)),
                  pl.BlockSpec((d, N_PROJ), lambda i: (0, 0)),
                  pl.BlockSpec((tm, LANES), lambda i: (i % nbl, 0)),
                  pl.BlockSpec((tm, LANES), lambda i: (i % nbl, 0))],
        out_specs=[pl.BlockSpec((tm, w), row) for w, _ in widths],
        out_shape=[jax.ShapeDtypeStruct((m, w), dt) for w, dt in widths],
        compiler_params=pltpu.CompilerParams(vmem_limit_bytes=VMEM_LIMIT),
        name="in_projection",
    )(x2, scale, shift, norm_g, w_r, cos_t, sin_t)


def _attn_kernel(q_ref, qi_ref, misc_ref, gatt_ref, kd_ref, kid_ref, vd_ref, tri_ref, o_ref,
                 sc_ref, qs_ref, qis_ref, sa_ref, sb_ref, p_ref, m_ref, l_ref, acc_ref,
                 *, qb, kc_w, cb_w, topk):
    i = pl.program_id(1)
    q0 = i * qb
    nkc = lax.div(q0 + qb + kc_w - 1, kc_w)
    lane = lax.broadcasted_iota(jnp.int32, (qb, LANES), 1)
    lo = lane < HEAD_DIM
    kf = float(topk)
    n_lg = kc_w // LANES
    rb = 64

    def stack_heads(dst_ref, src_ref, n_heads):
        for h in range(n_heads):
            grp = src_ref[:, (h // 2) * LANES:(h // 2 + 1) * LANES].astype(F32)
            keep = lo if h % 2 == 0 else jnp.logical_not(lo)
            dst_ref[h * qb:(h + 1) * qb, :] = jnp.where(keep, grp, 0.0).astype(BF16)

    def chunk_start(kc):
        return pl.multiple_of(kc * kc_w, kc_w)

    stack_heads(qis_ref, qi_ref, N_IDX_HEADS)
    n_blk = qb // LANES
    blk_lane = lax.broadcasted_iota(jnp.int32, (1, LANES), 1)
    w_rows = []
    for jb in range(n_blk):
        misc_t = misc_ref[jb * LANES:(jb + 1) * LANES, :].T
        w_rows.append([misc_t[WI_OFF + h:WI_OFF + h + 1, :] * N_IDX_HEADS ** -0.5 for h in range(N_IDX_HEADS)])

    def idx_body(kc, masked):
        k0 = chunk_start(kc)
        d = lax.dot_general(kid_ref[pl.ds(k0, kc_w), :], qis_ref[...], _NT, preferred_element_type=F32)
        for jb in range(n_blk):
            s = None
            for h in range(N_IDX_HEADS):
                c0 = h * qb + jb * LANES
                term = jnp.maximum(d[:, c0:c0 + LANES], 0.0) * w_rows[jb][h]
                s = term if s is None else s + term
            if masked:
                kpos = k0 + lax.broadcasted_iota(jnp.int32, (kc_w, 1), 0)
                s = jnp.where(kpos <= q0 + jb * LANES + blk_lane, s, NEG)
            sc_ref[jb, pl.ds(k0, kc_w), :] = s

    def idx_full(kc, carry):
        idx_body(kc, False)
        return carry

    lax.fori_loop(0, nkc - 1, idx_full, 0)
    idx_body(nkc - 1, True)

    n_cb = lax.div(q0 + qb + cb_w - 1, cb_w)
    ncols = (n_cb * cb_w).astype(F32)

    def count_below(cand):
        cand_b = [jnp.broadcast_to(cand[jb:jb + 1, :], (rb, LANES)) for jb in range(n_blk)]

        def body(kb, accs):
            k0 = pl.multiple_of(kb * cb_w, cb_w)
            out = []
            for jb in range(n_blk):
                acc = accs[jb]
                for r in range(cb_w // rb):
                    diff = sc_ref[jb, pl.ds(k0 + r * rb, rb), :] - cand_b[jb]
                    acc = acc + lax.shift_right_logical(pltpu.bitcast(diff, jnp.int32), 31)
                out.append(acc)
            return tuple(out)

        accs = lax.fori_loop(0, n_cb, body, tuple(jnp.zeros((rb, LANES), jnp.int32) for _ in range(n_blk)))
        return jnp.concatenate([jnp.sum(a.astype(F32), axis=0, keepdims=True) for a in accs], axis=0)

    def ordinal_to_float(c):
        bits = jnp.where(c > 0, c, jnp.int32(INT_MIN) - c)
        return pltpu.bitcast(bits, F32)

    cnt0 = ncols - count_below(jnp.full((n_blk, LANES), -0.0, F32))
    nonneg = cnt0 >= kf
    prefix0 = jnp.where(nonneg, 0, jnp.int32(INT_MIN))
    cnt_at0 = jnp.where(nonneg, cnt0, ncols)

    def bit_body(b, carry):
        prefix, cnt_at = carry
        cand = prefix | lax.shift_left(jnp.int32(1), 30 - b)
        cnt = ncols - count_below(ordinal_to_float(cand))
        ok = cnt >= kf
        return jnp.where(ok, cand, prefix), jnp.where(ok, cnt, cnt_at)

    prefix, cnt_at = lax.fori_loop(0, 31, bit_body, (prefix0, cnt_at0))
    tiny = jnp.logical_and(prefix > -(1 << 23), prefix < (1 << 23))
    thr = jnp.where(prefix == jnp.int32(INT_MIN), -jnp.inf,
                    jnp.where(tiny, 0.0, ordinal_to_float(prefix)))

    excess = cnt_at - kf
    tied = jnp.logical_and(excess > 0.0, thr > NEG)
    any_tied = jnp.max(jnp.where(tied, 1.0, 0.0)) > 0.5

    @pl.when(any_tied)
    def _():
        def drop(i, seen):
            k0 = chunk_start(nkc - 1 - i)
            out = []
            for jb in range(n_blk):
                after = seen[jb]
                for t in reversed(range(n_lg)):
                    ch = sc_ref[jb, pl.ds(k0 + t * LANES, LANES), :]
                    tie = ch == thr[jb:jb + 1, :]
                    rank = after + _dot(tri_ref[...], jnp.where(tie, 1.0, 0.0).astype(BF16))
                    dropped = jnp.logical_and(tie, rank <= excess[jb:jb + 1, :])
                    sc_ref[jb, pl.ds(k0 + t * LANES, LANES), :] = jnp.where(dropped, NEG, ch)
                    after = rank[0:1, :]
                out.append(after)
            return tuple(out)

        lax.fori_loop(0, nkc, drop, tuple(jnp.zeros((1, LANES), F32) for _ in range(n_blk)))

    thr = jnp.maximum(thr, NEG_UP)

    m_ref[...] = jnp.full(m_ref.shape, NEG, F32)
    l_ref[...] = jnp.zeros(l_ref.shape, F32)
    acc_ref[...] = jnp.zeros(acc_ref.shape, F32)
    stack_heads(qs_ref, q_ref, N_ATT_HEADS)

    def logits(kc, s_ref):
        k0 = chunk_start(kc)
        bias = []
        for t in range(n_lg):
            blocks = [jnp.where(sc_ref[jb, pl.ds(k0 + t * LANES, LANES), :] >= thr[jb:jb + 1, :], 0.0, NEG).T
                      for jb in range(n_blk)]
            bias.append(jnp.concatenate(blocks, axis=0))
        d = lax.dot_general(qs_ref[...], kd_ref[pl.ds(k0, kc_w), :], _NT, preferred_element_type=F32)
        for h in range(N_ATT_HEADS):
            for t in range(n_lg):
                s_ref[h * qb:(h + 1) * qb, t * LANES:(t + 1) * LANES] = (
                    d[h * qb:(h + 1) * qb, t * LANES:(t + 1) * LANES] + bias[t])

    def softmax_pv(kc, s_ref):
        k0 = chunk_start(kc)
        alphas = []
        for h in range(N_ATT_HEADS):
            rows = slice(h * qb, (h + 1) * qb)
            masked = lambda t: s_ref[rows, t * LANES:(t + 1) * LANES]
            mx = masked(0)
            for t in range(1, n_lg):
                mx = jnp.maximum(mx, masked(t))
            m_old = m_ref[h]
            m_new = jnp.maximum(m_old, jnp.max(mx, axis=-1, keepdims=True))
            alpha = jnp.exp2(m_old - m_new)
            m_ref[h] = m_new
            psum = None
            for t in range(n_lg):
                p = jnp.exp2(masked(t) - m_new)
                p_ref[rows, t * LANES:(t + 1) * LANES] = p.astype(BF16)
                psum = p if psum is None else psum + p
            l_ref[h] = alpha * l_ref[h] + psum
            alphas.append(alpha)
        pv = _dot(p_ref[...], vd_ref[pl.ds(k0, kc_w), :])
        for g in range(N_ATT_HEADS // 2):
            pair = jnp.where(lo, pv[2 * g * qb:(2 * g + 1) * qb], pv[(2 * g + 1) * qb:(2 * g + 2) * qb])
            acc_ref[g] = acc_ref[g] * jnp.where(lo, alphas[2 * g], alphas[2 * g + 1]) + pair

    logits(0, sa_ref)

    def att_pair(j, carry):
        kc = 2 * j
        logits(kc + 1, sb_ref)
        softmax_pv(kc, sa_ref)
        logits(jnp.minimum(kc + 2, nkc - 1), sa_ref)
        softmax_pv(kc + 1, sb_ref)
        return carry

    lax.fori_loop(0, lax.div(nkc, 2), att_pair, 0)

    @pl.when(lax.rem(nkc, 2) == 1)
    def _():
        softmax_pv(nkc - 1, sa_ref)

    outs = []
    for g in range(N_ATT_HEADS // 2):
        l_lo = jnp.sum(l_ref[2 * g], axis=-1, keepdims=True)
        l_hi = jnp.sum(l_ref[2 * g + 1], axis=-1, keepdims=True)
        outs.append(acc_ref[g] * jnp.where(lo, 1.0 / l_lo, 1.0 / l_hi))
    o_ref[...] = (jnp.concatenate(outs, axis=1) * _silu(gatt_ref[...])).astype(BF16)


def _attention(q, qi, misc, gatt, kd, kid, vd, bsz, seq_len):
    m = q.shape[0]
    qb = min(256, seq_len)
    kc_w = min(512, seq_len)
    topk = min(INDEX_TOPK_MAX, seq_len // 4)
    cb_w = kc_w
    assert seq_len % kc_w == 0 and kc_w % qb == 0 and qb % LANES == 0 and kc_w % cb_w == 0 and cb_w >= topk
    nqb = seq_len // qb
    qrow = lambda b, i: (b * nqb + i, 0)
    kv = lambda b, i: (b, 0)
    kern = functools.partial(_attn_kernel, qb=qb, kc_w=kc_w, cb_w=cb_w, topk=topk)
    tri = (jnp.arange(LANES)[:, None] <= jnp.arange(LANES)[None, :]).astype(BF16)
    return pl.pallas_call(
        kern,
        grid=(bsz, nqb),
        in_specs=[pl.BlockSpec((qb, ATT_W), qrow),
                  pl.BlockSpec((qb, N_IDX_HEADS * IDX_DIM), qrow),
                  pl.BlockSpec((qb, MISC_W), qrow),
                  pl.BlockSpec((qb, ATT_W), qrow),
                  pl.BlockSpec((seq_len, LANES), kv),
                  pl.BlockSpec((seq_len, LANES), kv),
                  pl.BlockSpec((seq_len, LANES), kv),
                  pl.BlockSpec((LANES, LANES), lambda b, i: (0, 0))],
        out_specs=pl.BlockSpec((qb, ATT_W), qrow),
        out_shape=jax.ShapeDtypeStruct((m, ATT_W), BF16),
        scratch_shapes=[pltpu.VMEM((qb // LANES, seq_len, LANES), F32),
                        pltpu.VMEM((N_ATT_HEADS * qb, LANES), BF16),
                        pltpu.VMEM((N_IDX_HEADS * qb, LANES), BF16),
                        pltpu.VMEM((N_ATT_HEADS * qb, kc_w), F32),
                        pltpu.VMEM((N_ATT_HEADS * qb, kc_w), F32),
                        pltpu.VMEM((N_ATT_HEADS * qb, kc_w), BF16),
                        pltpu.VMEM((N_ATT_HEADS, qb, LANES), F32),
                        pltpu.VMEM((N_ATT_HEADS, qb, LANES), F32),
                        pltpu.VMEM((N_ATT_HEADS // 2, qb, LANES), F32)],
        compiler_params=pltpu.CompilerParams(vmem_limit_bytes=VMEM_LIMIT,
                                             dimension_semantics=("arbitrary", "arbitrary")),
        name="dsa_attention",
    )(q, qi, misc, gatt, kd, kid, vd, tri)


def _ssd_kernel(xbc_ref, misc_ref, z_ref, cw_ref, cb_ref, dtb_ref, alog_ref, dsk_ref, ng_ref, e_ref,
                y_ref, ext_ref, st_ref):
    c = pl.program_id(1)
    gw = D_SSM // N_GROUPS
    tail = 8

    @pl.when(c == 0)
    def _():
        ext_ref[0:tail, :] = jnp.zeros((tail, CONV_CH), F32)
        st_ref[...] = jnp.zeros(st_ref.shape, F32)

    ext_ref[tail:tail + CHUNK, :] = xbc_ref[...]
    conv = cb_ref[...]
    for j in range(CONV_K):
        conv = conv + ext_ref[tail - j:tail - j + CHUNK, :] * cw_ref[CONV_K - 1 - j:CONV_K - j, :]
    ext_ref[0:tail, :] = ext_ref[CHUNK:CHUNK + tail, :]
    u = _silu(conv)
    xs = u[:, 0:D_SSM]
    bm = u[:, D_SSM:D_SSM + N_GROUPS * D_STATE]
    cm = u[:, D_SSM + N_GROUPS * D_STATE:]

    row = lax.broadcasted_iota(jnp.int32, (CHUNK, LANES), 0)
    lane = lax.broadcasted_iota(jnp.int32, (CHUNK, LANES), 1)
    tril = row >= lane
    lo = lane < SSM_HEAD_DIM

    dtr = misc_ref[...] + dtb_ref[...]
    softplus = jnp.maximum(dtr, 0.0) + jnp.log1p(jnp.exp(-jnp.abs(dtr)))
    dtv = jnp.where(lane < N_SSM_HEADS, softplus, 0.0)
    adt = dtv * (-jnp.exp(alog_ref[...]))
    acs = _dot_sel(jnp.where(tril, 1.0, 0.0).astype(BF16), adt)
    acs_t = acs.T
    expand = e_ref[...]
    dt_e = _dot_sel(dtv, expand)
    acs_e = _dot_sel(acs, expand)
    last_e = acs_e[CHUNK - 1:CHUNK, :]
    xd = xs * dt_e
    w_state = (xd * jnp.exp(last_e - acs_e)).astype(BF16)
    decay_in = jnp.exp(acs_e)
    chunk_decay = jnp.exp(last_e)

    ys = []
    for g in range(N_GROUPS):
        bm_g = bm[:, g * D_STATE:(g + 1) * D_STATE]
        cmb = cm[:, g * D_STATE:(g + 1) * D_STATE].astype(BF16)
        cb = lax.dot_general(cmb, bm_g.astype(BF16), _NT, preferred_element_type=F32)
        st = st_ref[g]
        y_off = _dot(cmb, st.astype(BF16)) * decay_in[:, g * gw:(g + 1) * gw]
        st_ref[g] = st * chunk_decay[:, g * gw:(g + 1) * gw] + _dot(bm_g.T.astype(BF16), w_state[:, g * gw:(g + 1) * gw])
        for p in range(gw // LANES):
            pair = g * (gw // LANES) + p
            xp = xd[:, pair * LANES:(pair + 1) * LANES]
            acc = y_off[:, p * LANES:(p + 1) * LANES]
            for half in range(2):
                h = 2 * pair + half
                seg = acs[:, h:h + 1] - acs_t[h:h + 1, :]
                decay = jnp.exp(jnp.where(tril, seg, -jnp.inf))
                xh = jnp.where(lo if half == 0 else jnp.logical_not(lo), xp, 0.0).astype(BF16)
                acc = acc + _dot((cb * decay).astype(BF16), xh)
            ys.append(acc)
    y = jnp.concatenate(ys, axis=1) + dsk_ref[...] * xs
    y = y * _silu(z_ref[...])
    normed = []
    for g in range(N_GROUPS):
        yg = y[:, g * gw:(g + 1) * gw]
        normed.append(yg * lax.rsqrt(jnp.mean(yg * yg, axis=-1, keepdims=True) + EPS))
    y_ref[...] = (jnp.concatenate(normed, axis=1) * ng_ref[...]).astype(BF16)


def _ssd(xbc, misc, z, conv_w, conv_b, dt_bias, a_log, d_skip, ssm_norm_g, expand, bsz, seq_len):
    m = xbc.shape[0]
    nc = seq_len // CHUNK
    rowc = lambda b, c: (b * nc + c, 0)
    const = lambda b, c: (0, 0)
    pad = lambda v: jnp.pad(v, (0, LANES - v.shape[0])).reshape(1, LANES)
    return pl.pallas_call(
        _ssd_kernel,
        grid=(bsz, nc),
        in_specs=[pl.BlockSpec((CHUNK, CONV_CH), rowc),
                  pl.BlockSpec((CHUNK, MISC_W), rowc),
                  pl.BlockSpec((CHUNK, D_SSM), rowc),
                  pl.BlockSpec((CONV_K, CONV_CH), const),
                  pl.BlockSpec((1, CONV_CH), const),
                  pl.BlockSpec((1, LANES), const),
                  pl.BlockSpec((1, LANES), const),
                  pl.BlockSpec((1, D_SSM), const),
                  pl.BlockSpec((1, D_SSM), const),
                  pl.BlockSpec((LANES, D_SSM), const)],
        out_specs=pl.BlockSpec((CHUNK, D_SSM), rowc),
        out_shape=jax.ShapeDtypeStruct((m, D_SSM), BF16),
        scratch_shapes=[pltpu.VMEM((CHUNK + 8, CONV_CH), F32),
                        pltpu.VMEM((N_GROUPS, D_STATE, D_SSM // N_GROUPS), F32)],
        compiler_params=pltpu.CompilerParams(vmem_limit_bytes=VMEM_LIMIT,
                                             dimension_semantics=("arbitrary", "arbitrary")),
        name="ssd_mixer",
    )(xbc, misc, z, conv_w, conv_b.reshape(1, CONV_CH), pad(dt_bias), pad(a_log),
      jnp.repeat(d_skip, SSM_HEAD_DIM).reshape(1, D_SSM), ssm_norm_g.reshape(1, D_SSM), expand)


def _merge_kernel(x_ref, oa_ref, y_ref, gla_ref, gls_ref, gate_ref, wa_ref, ws_ref, wo_ref, fg_ref, o_ref,
                  *, final_norm):
    y_att = _dot(oa_ref[...], wa_ref[...])
    y_ssd = _dot(y_ref[...], ws_ref[...])
    merged = jax.nn.sigmoid(gla_ref[...]) * y_att + jax.nn.sigmoid(gls_ref[...]) * y_ssd
    out = x_ref[...] + gate_ref[0] * _dot(merged.astype(BF16), wo_ref[...])
    if final_norm:
        out = out * lax.rsqrt(jnp.mean(out * out, axis=-1, keepdims=True) + EPS) * fg_ref[...]
    o_ref[...] = out


def _merge(x2, oa, y, gla, gls, gate, w_a, w_s, w_o, final_g, seq_len, final_norm):
    m, d = x2.shape
    tm = 512
    nbl = seq_len // tm
    row = lambda i: (i, 0)
    const = lambda i: (0, 0)
    return pl.pallas_call(
        functools.partial(_merge_kernel, final_norm=final_norm),
        grid=(m // tm,),
        in_specs=[pl.BlockSpec((tm, d), row),
                  pl.BlockSpec((tm, ATT_W), row),
                  pl.BlockSpec((tm, D_SSM), row),
                  pl.BlockSpec((tm, d), row),
                  pl.BlockSpec((tm, d), row),
                  pl.BlockSpec((1, 1, d), lambda i: (i // nbl, 0, 0)),
                  pl.BlockSpec((ATT_W, d), const),
                  pl.BlockSpec((D_SSM, d), const),
                  pl.BlockSpec((d, d), const),
                  pl.BlockSpec((1, d), const)],
        out_specs=pl.BlockSpec((tm, d), row),
        out_shape=jax.ShapeDtypeStruct((m, d), F32),
        compiler_params=pltpu.CompilerParams(vmem_limit_bytes=VMEM_LIMIT),
        name="merge_out_projection",
    )(x2, oa, y, gla, gls, gate, w_a, w_s, w_o, final_g)


def _relayout_w_in(w_in):
    w_in = w_in.astype(BF16)
    offs = np.cumsum((0,) + IN_SIZES)
    seg = [w_in[..., int(offs[i]):int(offs[i + 1])] for i in range(len(IN_SIZES))]
    q, k, v, g_att, qi, ki, wi, z, xbc, dt, gla, gls = seg
    zeros = lambda n: jnp.zeros(w_in.shape[:-1] + (n,), w_in.dtype)
    cols = [q, k, k, ki, ki, qi,
            v, v,
            g_att, dt, wi, zeros(MISC_W - N_SSM_HEADS - N_IDX_HEADS), z, xbc, gla, gls]
    return jnp.concatenate(cols, axis=-1)


def _rope_tables(seq_len):
    inv = ROPE_THETA ** (-jnp.arange(0, HEAD_DIM, 2, dtype=F32) / HEAD_DIM)
    ang = jnp.arange(seq_len, dtype=F32)[:, None] * inv[None, :]
    cos, sin = jnp.cos(ang), jnp.sin(ang)
    return (jnp.concatenate([cos, cos, cos, cos], axis=-1),
            jnp.concatenate([-sin, sin, -sin, sin], axis=-1))


def kernel(x, c, w_ada, b_ada, norm_g, w_in, conv_w, conv_b, dt_bias, a_log, d_skip, ssm_norm_g,
           w_branch_a, w_branch_s, w_out, final_g):
    bsz, seq_len, d = x.shape
    depth = w_in.shape[0]
    assert d == D_MODEL and seq_len % CHUNK == 0
    mod = _modulation(c, w_ada, b_ada)
    w_r = _relayout_w_in(w_in)
    w_a = w_branch_a.astype(BF16)
    w_s = w_branch_s.astype(BF16)
    w_o = w_out.astype(BF16)
    cos_t, sin_t = _rope_tables(seq_len)
    expand = (jnp.arange(LANES)[:, None] == (jnp.arange(D_SSM) // SSM_HEAD_DIM)[None, :]).astype(BF16)
    fg = final_g.reshape(1, d)

    x2 = x.reshape(bsz * seq_len, d)
    for i in range(depth):
        shift = mod[i, :, 0:d].reshape(bsz, 1, d)
        scale = mod[i, :, d:2 * d].reshape(bsz, 1, d)
        gate = mod[i, :, 2 * d:].reshape(bsz, 1, d)
        (q, kd, kid, qi, vd, gatt, misc, z, xbc, gla, gls) = _in_projection(
            x2, scale, shift, norm_g[i].reshape(1, d), w_r[i], cos_t, sin_t, seq_len)
        oa = _attention(q, qi, misc, gatt, kd, kid, vd, bsz, seq_len)
        y = _ssd(xbc, misc, z, conv_w[i], conv_b[i], dt_bias[i], a_log[i], d_skip[i], ssm_norm_g[i],
                 expand, bsz, seq_len)
        x2 = _merge(x2, oa, y, gla, gls, gate, w_a[i], w_s[i], w_o[i], fg, seq_len, i == depth - 1)
    return x2.reshape(bsz, seq_len, d)
```

```python
import functools
import math

import numpy as np
import jax
import jax.numpy as jnp
from jax import lax
from jax.experimental import pallas as pl
from jax.experimental.pallas import tpu as pltpu

D_MODEL = 1024
N_ATT_HEADS = 8
HEAD_DIM = 64
ATT_W = N_ATT_HEADS * HEAD_DIM
N_IDX_HEADS = 4
IDX_DIM = HEAD_DIM
INDEX_TOPK_MAX = 256
D_SSM = D_MODEL
SSM_HEAD_DIM = 64
N_SSM_HEADS = D_SSM // SSM_HEAD_DIM
N_GROUPS = 2
D_STATE = 128
CONV_K = 4
CHUNK = 128
CONV_CH = D_SSM + 2 * N_GROUPS * D_STATE
ROPE_THETA = 10000.0
EPS = 1e-6
NEG = -1e30
IN_SIZES = (ATT_W, HEAD_DIM, HEAD_DIM, ATT_W, N_IDX_HEADS * IDX_DIM, IDX_DIM, N_IDX_HEADS,
            D_SSM, CONV_CH, N_SSM_HEADS, D_MODEL, D_MODEL)

LANES = 128
ROPE_W = 1024
MISC_W = LANES
WI_OFF = N_SSM_HEADS
LOG2E = math.log2(math.e)
N_PROJ = ROPE_W + LANES + ATT_W + MISC_W + D_SSM + CONV_CH + 2 * D_MODEL
VMEM_LIMIT = 56 * 1024 * 1024
INT_MIN = -2 ** 31
NEG_UP = float(np.nextafter(np.float32(NEG), np.float32(0.0)))

F32 = jnp.float32
BF16 = jnp.bfloat16
_NT = (((1,), (1,)), ((), ()))


def _silu(v):
    return v * jax.nn.sigmoid(v)


def _dot(a, b):
    return jnp.dot(a, b, preferred_element_type=F32)


def _dot_exact(a, b):
    return jnp.dot(a, b, preferred_element_type=F32, precision=lax.Precision.HIGHEST)


def _split3(x):
    hi = x.astype(BF16)
    rest = x - hi.astype(F32)
    mid = rest.astype(BF16)
    return hi, mid, (rest - mid.astype(F32)).astype(BF16)


def _dot_sel(a, b):
    if a.dtype == BF16:
        return sum(_dot(a, piece) for piece in _split3(b))
    return sum(_dot(piece, b) for piece in _split3(a))


def _mod_kernel(c_ref, w_ref, b_ref, o_ref):
    o_ref[0] = _dot_exact(_silu(c_ref[...]), w_ref[0]) + b_ref[0]


def _modulation(c, w_ada, b_ada):
    depth, d, n = w_ada.shape
    bsz = c.shape[0]
    tn = 1024
    return pl.pallas_call(
        _mod_kernel,
        grid=(depth, n // tn),
        in_specs=[pl.BlockSpec((bsz, d), lambda i, j: (0, 0)),
                  pl.BlockSpec((1, d, tn), lambda i, j: (i, 0, j)),
                  pl.BlockSpec((1, 1, tn), lambda i, j: (i, 0, j))],
        out_specs=pl.BlockSpec((1, bsz, tn), lambda i, j: (i, 0, j)),
        out_shape=jax.ShapeDtypeStruct((depth, bsz, n), F32),
        compiler_params=pltpu.CompilerParams(vmem_limit_bytes=VMEM_LIMIT),
        name="adaln_modulation",
    )(c, w_ada, b_ada.reshape(depth, 1, n))


def _inproj_kernel(x_ref, scale_ref, shift_ref, g_ref, w_ref, cos_ref, sin_ref,
                   q_ref, kd_ref, kid_ref, qi_ref, vd_ref, gatt_ref, misc_ref,
                   z_ref, xbc_ref, gla_ref, gls_ref):
    x = x_ref[...]
    h = x * lax.rsqrt(jnp.mean(x * x, axis=-1, keepdims=True) + EPS) * g_ref[...]
    h = h * (1.0 + scale_ref[0]) + shift_ref[0]
    hb = h.astype(BF16)

    def proj(a, b):
        return _dot(hb, w_ref[:, a:b])

    r = proj(0, ROPE_W)
    cos = cos_ref[...]
    sin = sin_ref[...]
    lane = lax.broadcasted_iota(jnp.int32, cos.shape, 1)
    first_half = (lane % HEAD_DIM) < (HEAD_DIM // 2)
    roped = []
    for gi in range(ROPE_W // LANES):
        u = r[:, gi * LANES:(gi + 1) * LANES]
        rot = jnp.where(first_half, pltpu.roll(u, LANES - HEAD_DIM // 2, 1), pltpu.roll(u, HEAD_DIM // 2, 1))
        roped.append(u * cos + rot * sin)
    q_ref[...] = (jnp.concatenate(roped[0:4], axis=1) * (HEAD_DIM ** -0.5 * LOG2E)).astype(BF16)
    kd_ref[...] = roped[4].astype(BF16)
    kid_ref[...] = roped[5].astype(BF16)
    qi_ref[...] = (jnp.concatenate(roped[6:8], axis=1) * IDX_DIM ** -0.5).astype(BF16)

    o = ROPE_W
    vd_ref[...] = proj(o, o + LANES).astype(BF16)
    o += LANES
    gatt_ref[...] = proj(o, o + ATT_W)
    o += ATT_W
    misc_ref[...] = proj(o, o + MISC_W)
    o += MISC_W
    z_ref[...] = proj(o, o + D_SSM)
    o += D_SSM
    xbc_ref[...] = proj(o, o + CONV_CH)
    o += CONV_CH
    gla_ref[...] = proj(o, o + D_MODEL)
    o += D_MODEL
    gls_ref[...] = proj(o, o + D_MODEL)


def _in_projection(x2, scale, shift, norm_g, w_r, layer, cos_t, sin_t, seq_len):
    m, d = x2.shape
    tm = 256
    nbl = seq_len // tm
    row = lambda i: (i, 0)
    widths = [(ATT_W, BF16), (LANES, BF16), (LANES, BF16), (N_IDX_HEADS * IDX_DIM, BF16),
              (LANES, BF16), (ATT_W, F32), (MISC_W, F32),
              (D_SSM, F32), (CONV_CH, F32), (D_MODEL, F32), (D_MODEL, F32)]
    return pl.pallas_call(
        _inproj_kernel,
        grid=(m // tm,),
        in_specs=[pl.BlockSpec((tm, d), row),
                  pl.BlockSpec((1, 1, d), lambda i: (i // nbl, 0, 0)),
                  pl.BlockSpec((1, 1, d), lambda i: (i // nbl, 0, 0)),
                  pl.BlockSpec((1, d), lambda i: (0, 0)),
                  pl.BlockSpec((None, d, N_PROJ), lambda i: (layer, 0, 0)),
                  pl.BlockSpec((tm, LANES), lambda i: (i % nbl, 0)),
                  pl.BlockSpec((tm, LANES), lambda i: (i % nbl, 0))],
        out_specs=[pl.BlockSpec((tm, w), row) for w, _ in widths],
        out_shape=[jax.ShapeDtypeStruct((m, w), dt) for w, dt in widths],
        compiler_params=pltpu.CompilerParams(vmem_limit_bytes=VMEM_LIMIT),
        name="in_projection",
    )(x2, scale, shift, norm_g, w_r, cos_t, sin_t)


def _attn_kernel(q_ref, qi_ref, misc_ref, gatt_ref, kd_ref, kid_ref, vd_ref, tri_ref, o_ref,
                 sc_ref, qs_ref, qis_ref, sa_ref, sb_ref, p_ref, m_ref, l_ref, acc_ref,
                 *, qb, kc_w, cb_w, topk):
    i = pl.program_id(1)
    q0 = i * qb
    nkc = lax.div(q0 + qb + kc_w - 1, kc_w)
    lane = lax.broadcasted_iota(jnp.int32, (qb, LANES), 1)
    lo = lane < HEAD_DIM
    kf = float(topk)
    n_lg = kc_w // LANES
    rb = 64

    def stack_heads(dst_ref, src_ref, n_heads):
        for h in range(n_heads):
            grp = src_ref[:, (h // 2) * LANES:(h // 2 + 1) * LANES].astype(F32)
            keep = lo if h % 2 == 0 else jnp.logical_not(lo)
            dst_ref[h * qb:(h + 1) * qb, :] = jnp.where(keep, grp, 0.0).astype(BF16)

    def chunk_start(kc):
        return pl.multiple_of(kc * kc_w, kc_w)

    stack_heads(qis_ref, qi_ref, N_IDX_HEADS)
    n_blk = qb // LANES
    blk_lane = lax.broadcasted_iota(jnp.int32, (1, LANES), 1)
    w_rows = []
    for jb in range(n_blk):
        misc_t = misc_ref[jb * LANES:(jb + 1) * LANES, :].T
        w_rows.append([misc_t[WI_OFF + h:WI_OFF + h + 1, :] * N_IDX_HEADS ** -0.5 for h in range(N_IDX_HEADS)])

    def idx_body(kc, masked):
        k0 = chunk_start(kc)
        d = lax.dot_general(kid_ref[pl.ds(k0, kc_w), :], qis_ref[...], _NT, preferred_element_type=F32)
        for jb in range(n_blk):
            s = None
            for h in range(N_IDX_HEADS):
                c0 = h * qb + jb * LANES
                term = jnp.maximum(d[:, c0:c0 + LANES], 0.0) * w_rows[jb][h]
                s = term if s is None else s + term
            if masked:
                kpos = k0 + lax.broadcasted_iota(jnp.int32, (kc_w, 1), 0)
                s = jnp.where(kpos <= q0 + jb * LANES + blk_lane, s, NEG)
            sc_ref[jb, pl.ds(k0, kc_w), :] = s

    def idx_full(kc, carry):
        idx_body(kc, False)
        return carry

    lax.fori_loop(0, nkc - 1, idx_full, 0)
    idx_body(nkc - 1, True)

    n_cb = lax.div(q0 + qb + cb_w - 1, cb_w)
    ncols = (n_cb * cb_w).astype(F32)

    def count_below(cand):
        cand_b = [jnp.broadcast_to(cand[jb:jb + 1, :], (rb, LANES)) for jb in range(n_blk)]

        def body(kb, accs):
            k0 = pl.multiple_of(kb * cb_w, cb_w)
            out = []
            for jb in range(n_blk):
                acc = accs[jb]
                for r in range(cb_w // rb):
                    diff = sc_ref[jb, pl.ds(k0 + r * rb, rb), :] - cand_b[jb]
                    acc = acc + lax.shift_right_logical(pltpu.bitcast(diff, jnp.int32), 31)
                out.append(acc)
            return tuple(out)

        accs = lax.fori_loop(0, n_cb, body, tuple(jnp.zeros((rb, LANES), jnp.int32) for _ in range(n_blk)))
        return jnp.concatenate([jnp.sum(a.astype(F32), axis=0, keepdims=True) for a in accs], axis=0)

    def ordinal_to_float(c):
        bits = jnp.where(c > 0, c, jnp.int32(INT_MIN) - c)
        return pltpu.bitcast(bits, F32)

    cnt0 = ncols - count_below(jnp.full((n_blk, LANES), -0.0, F32))
    nonneg = cnt0 >= kf
    prefix0 = jnp.where(nonneg, 0, jnp.int32(INT_MIN))
    cnt_at0 = jnp.where(nonneg, cnt0, ncols)

    def bit_body(b, carry):
        prefix, cnt_at = carry
        cand = prefix | lax.shift_left(jnp.int32(1), 30 - b)
        cnt = ncols - count_below(ordinal_to_float(cand))
        ok = cnt >= kf
        return jnp.where(ok, cand, prefix), jnp.where(ok, cnt, cnt_at)

    prefix, cnt_at = lax.fori_loop(0, 31, bit_body, (prefix0, cnt_at0))
    tiny = jnp.logical_and(prefix > -(1 << 23), prefix < (1 << 23))
    thr = jnp.where(prefix == jnp.int32(INT_MIN), -jnp.inf,
                    jnp.where(tiny, 0.0, ordinal_to_float(prefix)))

    excess = cnt_at - kf
    tied = jnp.logical_and(excess > 0.0, thr > NEG)
    any_tied = jnp.max(jnp.where(tied, 1.0, 0.0)) > 0.5

    @pl.when(any_tied)
    def _():
        def drop(i, seen):
            k0 = chunk_start(nkc - 1 - i)
            out = []
            for jb in range(n_blk):
                after = seen[jb]
                for t in reversed(range(n_lg)):
                    ch = sc_ref[jb, pl.ds(k0 + t * LANES, LANES), :]
                    tie = ch == thr[jb:jb + 1, :]
                    rank = after + _dot(tri_ref[...], jnp.where(tie, 1.0, 0.0).astype(BF16))
                    dropped = jnp.logical_and(tie, rank <= excess[jb:jb + 1, :])
                    sc_ref[jb, pl.ds(k0 + t * LANES, LANES), :] = jnp.where(dropped, NEG, ch)
                    after = rank[0:1, :]
                out.append(after)
            return tuple(out)

        lax.fori_loop(0, nkc, drop, tuple(jnp.zeros((1, LANES), F32) for _ in range(n_blk)))

    thr = jnp.maximum(thr, NEG_UP)

    m_ref[...] = jnp.full(m_ref.shape, NEG, F32)
    l_ref[...] = jnp.zeros(l_ref.shape, F32)
    acc_ref[...] = jnp.zeros(acc_ref.shape, F32)
    stack_heads(qs_ref, q_ref, N_ATT_HEADS)

    def logits(kc, s_ref):
        k0 = chunk_start(kc)
        bias = []
        for t in range(n_lg):
            blocks = [jnp.where(sc_ref[jb, pl.ds(k0 + t * LANES, LANES), :] >= thr[jb:jb + 1, :], 0.0, NEG).T
                      for jb in range(n_blk)]
            bias.append(jnp.concatenate(blocks, axis=0))
        d = lax.dot_general(qs_ref[...], kd_ref[pl.ds(k0, kc_w), :], _NT, preferred_element_type=F32)
        for h in range(N_ATT_HEADS):
            for t in range(n_lg):
                s_ref[h * qb:(h + 1) * qb, t * LANES:(t + 1) * LANES] = (
                    d[h * qb:(h + 1) * qb, t * LANES:(t + 1) * LANES] + bias[t])

    def softmax_pv(kc, s_ref):
        k0 = chunk_start(kc)
        alphas = []
        for h in range(N_ATT_HEADS):
            rows = slice(h * qb, (h + 1) * qb)
            masked = lambda t: s_ref[rows, t * LANES:(t + 1) * LANES]
            mx = masked(0)
            for t in range(1, n_lg):
                mx = jnp.maximum(mx, masked(t))
            m_old = m_ref[h]
            m_new = jnp.maximum(m_old, jnp.max(mx, axis=-1, keepdims=True))
            alpha = jnp.exp2(m_old - m_new)
            m_ref[h] = m_new
            psum = None
            for t in range(n_lg):
                p = jnp.exp2(masked(t) - m_new)
                p_ref[rows, t * LANES:(t + 1) * LANES] = p.astype(BF16)
                psum = p if psum is None else psum + p
            l_ref[h] = alpha * l_ref[h] + psum
            alphas.append(alpha)
        pv = _dot(p_ref[...], vd_ref[pl.ds(k0, kc_w), :])
        for g in range(N_ATT_HEADS // 2):
            pair = jnp.where(lo, pv[2 * g * qb:(2 * g + 1) * qb], pv[(2 * g + 1) * qb:(2 * g + 2) * qb])
            acc_ref[g] = acc_ref[g] * jnp.where(lo, alphas[2 * g], alphas[2 * g + 1]) + pair

    logits(0, sa_ref)

    def att_pair(j, carry):
        kc = 2 * j
        logits(kc + 1, sb_ref)
        softmax_pv(kc, sa_ref)
        logits(jnp.minimum(kc + 2, nkc - 1), sa_ref)
        softmax_pv(kc + 1, sb_ref)
        return carry

    lax.fori_loop(0, lax.div(nkc, 2), att_pair, 0)

    @pl.when(lax.rem(nkc, 2) == 1)
    def _():
        softmax_pv(nkc - 1, sa_ref)

    outs = []
    for g in range(N_ATT_HEADS // 2):
        l_lo = jnp.sum(l_ref[2 * g], axis=-1, keepdims=True)
        l_hi = jnp.sum(l_ref[2 * g + 1], axis=-1, keepdims=True)
        outs.append(acc_ref[g] * jnp.where(lo, 1.0 / l_lo, 1.0 / l_hi))
    o_ref[...] = (jnp.concatenate(outs, axis=1) * _silu(gatt_ref[...])).astype(BF16)


def _attention(q, qi, misc, gatt, kd, kid, vd, bsz, seq_len):
    m = q.shape[0]
    qb = min(512, seq_len)
    kc_w = min(512, seq_len)
    topk = min(INDEX_TOPK_MAX, seq_len // 4)
    cb_w = kc_w
    assert seq_len % kc_w == 0 and kc_w % qb == 0 and qb % LANES == 0 and kc_w % cb_w == 0 and cb_w >= topk
    nqb = seq_len // qb
    qrow = lambda b, i: (b * nqb + i, 0)
    kv = lambda b, i: (b, 0)
    kern = functools.partial(_attn_kernel, qb=qb, kc_w=kc_w, cb_w=cb_w, topk=topk)
    tri = (jnp.arange(LANES)[:, None] <= jnp.arange(LANES)[None, :]).astype(BF16)
    return pl.pallas_call(
        kern,
        grid=(bsz, nqb),
        in_specs=[pl.BlockSpec((qb, ATT_W), qrow),
                  pl.BlockSpec((qb, N_IDX_HEADS * IDX_DIM), qrow),
                  pl.BlockSpec((qb, MISC_W), qrow),
                  pl.BlockSpec((qb, ATT_W), qrow),
                  pl.BlockSpec((seq_len, LANES), kv),
                  pl.BlockSpec((seq_len, LANES), kv),
                  pl.BlockSpec((seq_len, LANES), kv),
                  pl.BlockSpec((LANES, LANES), lambda b, i: (0, 0))],
        out_specs=pl.BlockSpec((qb, ATT_W), qrow),
        out_shape=jax.ShapeDtypeStruct((m, ATT_W), BF16),
        scratch_shapes=[pltpu.VMEM((qb // LANES, seq_len, LANES), F32),
                        pltpu.VMEM((N_ATT_HEADS * qb, LANES), BF16),
                        pltpu.VMEM((N_IDX_HEADS * qb, LANES), BF16),
                        pltpu.VMEM((N_ATT_HEADS * qb, kc_w), F32),
                        pltpu.VMEM((N_ATT_HEADS * qb, kc_w), F32),
                        pltpu.VMEM((N_ATT_HEADS * qb, kc_w), BF16),
                        pltpu.VMEM((N_ATT_HEADS, qb, LANES), F32),
                        pltpu.VMEM((N_ATT_HEADS, qb, LANES), F32),
                        pltpu.VMEM((N_ATT_HEADS // 2, qb, LANES), F32)],
        compiler_params=pltpu.CompilerParams(vmem_limit_bytes=VMEM_LIMIT,
                                             dimension_semantics=("arbitrary", "arbitrary")),
        name="dsa_attention",
    )(q, qi, misc, gatt, kd, kid, vd, tri)


def _ssd_kernel(xbc_ref, misc_ref, z_ref, cw_ref, cb_ref, dtb_ref, alog_ref, dsk_ref, ng_ref, e_ref,
                y_ref, ext_ref, st_ref):
    c = pl.program_id(1)
    gw = D_SSM // N_GROUPS
    tail = 8

    @pl.when(c == 0)
    def _():
        ext_ref[0:tail, :] = jnp.zeros((tail, CONV_CH), F32)
        st_ref[...] = jnp.zeros(st_ref.shape, F32)

    ext_ref[tail:tail + CHUNK, :] = xbc_ref[...]
    conv = cb_ref[...]
    for j in range(CONV_K):
        conv = conv + ext_ref[tail - j:tail - j + CHUNK, :] * cw_ref[CONV_K - 1 - j:CONV_K - j, :]
    ext_ref[0:tail, :] = ext_ref[CHUNK:CHUNK + tail, :]
    u = _silu(conv)
    xs = u[:, 0:D_SSM]
    bm = u[:, D_SSM:D_SSM + N_GROUPS * D_STATE]
    cm = u[:, D_SSM + N_GROUPS * D_STATE:]

    row = lax.broadcasted_iota(jnp.int32, (CHUNK, LANES), 0)
    lane = lax.broadcasted_iota(jnp.int32, (CHUNK, LANES), 1)
    tril = row >= lane
    lo = lane < SSM_HEAD_DIM

    dtr = misc_ref[...] + dtb_ref[...]
    softplus = jnp.maximum(dtr, 0.0) + jnp.log1p(jnp.exp(-jnp.abs(dtr)))
    dtv = jnp.where(lane < N_SSM_HEADS, softplus, 0.0)
    adt = dtv * (-jnp.exp(alog_ref[...]))
    acs = _dot_sel(jnp.where(tril, 1.0, 0.0).astype(BF16), adt)
    acs_t = acs.T
    expand = e_ref[...]
    dt_e = _dot_sel(dtv, expand)
    acs_e = _dot_sel(acs, expand)
    last_e = acs_e[CHUNK - 1:CHUNK, :]
    xd = xs * dt_e
    w_state = (xd * jnp.exp(last_e - acs_e)).astype(BF16)
    decay_in = jnp.exp(acs_e)
    chunk_decay = jnp.exp(last_e)

    ys = []
    for g in range(N_GROUPS):
        bm_g = bm[:, g * D_STATE:(g + 1) * D_STATE]
        cmb = cm[:, g * D_STATE:(g + 1) * D_STATE].astype(BF16)
        cb = lax.dot_general(cmb, bm_g.astype(BF16), _NT, preferred_element_type=F32)
        st = st_ref[g]
        y_off = _dot(cmb, st.astype(BF16)) * decay_in[:, g * gw:(g + 1) * gw]
        st_ref[g] = st * chunk_decay[:, g * gw:(g + 1) * gw] + _dot(bm_g.T.astype(BF16), w_state[:, g * gw:(g + 1) * gw])
        for p in range(gw // LANES):
            pair = g * (gw // LANES) + p
            xp = xd[:, pair * LANES:(pair + 1) * LANES]
            acc = y_off[:, p * LANES:(p + 1) * LANES]
            for half in range(2):
                h = 2 * pair + half
                seg = acs[:, h:h + 1] - acs_t[h:h + 1, :]
                decay = jnp.exp(jnp.where(tril, seg, -jnp.inf))
                xh = jnp.where(lo if half == 0 else jnp.logical_not(lo), xp, 0.0).astype(BF16)
                acc = acc + _dot((cb * decay).astype(BF16), xh)
            ys.append(acc)
    y = jnp.concatenate(ys, axis=1) + dsk_ref[...] * xs
    y = y * _silu(z_ref[...])
    normed = []
    for g in range(N_GROUPS):
        yg = y[:, g * gw:(g + 1) * gw]
        normed.append(yg * lax.rsqrt(jnp.mean(yg * yg, axis=-1, keepdims=True) + EPS))
    y_ref[...] = (jnp.concatenate(normed, axis=1) * ng_ref[...]).astype(BF16)


def _ssd(xbc, misc, z, conv_w, conv_b, dt_bias, a_log, d_skip, ssm_norm_g, expand, bsz, seq_len):
    m = xbc.shape[0]
    nc = seq_len // CHUNK
    rowc = lambda b, c: (b * nc + c, 0)
    const = lambda b, c: (0, 0)
    pad = lambda v: jnp.pad(v, (0, LANES - v.shape[0])).reshape(1, LANES)
    return pl.pallas_call(
        _ssd_kernel,
        grid=(bsz, nc),
        in_specs=[pl.BlockSpec((CHUNK, CONV_CH), rowc),
                  pl.BlockSpec((CHUNK, MISC_W), rowc),
                  pl.BlockSpec((CHUNK, D_SSM), rowc),
                  pl.BlockSpec((CONV_K, CONV_CH), const),
                  pl.BlockSpec((1, CONV_CH), const),
                  pl.BlockSpec((1, LANES), const),
                  pl.BlockSpec((1, LANES), const),
                  pl.BlockSpec((1, D_SSM), const),
                  pl.BlockSpec((1, D_SSM), const),
                  pl.BlockSpec((LANES, D_SSM), const)],
        out_specs=pl.BlockSpec((CHUNK, D_SSM), rowc),
        out_shape=jax.ShapeDtypeStruct((m, D_SSM), BF16),
        scratch_shapes=[pltpu.VMEM((CHUNK + 8, CONV_CH), F32),
                        pltpu.VMEM((N_GROUPS, D_STATE, D_SSM // N_GROUPS), F32)],
        compiler_params=pltpu.CompilerParams(vmem_limit_bytes=VMEM_LIMIT,
                                             dimension_semantics=("arbitrary", "arbitrary")),
        name="ssd_mixer",
    )(xbc, misc, z, conv_w, conv_b.reshape(1, CONV_CH), pad(dt_bias), pad(a_log),
      jnp.repeat(d_skip, SSM_HEAD_DIM).reshape(1, D_SSM), ssm_norm_g.reshape(1, D_SSM), expand)


def _merge_kernel(x_ref, oa_ref, y_ref, gla_ref, gls_ref, gate_ref, wa_ref, ws_ref, wo_ref, fg_ref, o_ref,
                  *, final_norm):
    y_att = _dot(oa_ref[...], wa_ref[...])
    y_ssd = _dot(y_ref[...], ws_ref[...])
    merged = jax.nn.sigmoid(gla_ref[...]) * y_att + jax.nn.sigmoid(gls_ref[...]) * y_ssd
    out = x_ref[...] + gate_ref[0] * _dot(merged.astype(BF16), wo_ref[...])
    if final_norm:
        out = out * lax.rsqrt(jnp.mean(out * out, axis=-1, keepdims=True) + EPS) * fg_ref[...]
    o_ref[...] = out


def _merge(x2, oa, y, gla, gls, gate, w_a, w_s, w_o, layer, final_g, seq_len, final_norm):
    m, d = x2.shape
    tm = 512
    nbl = seq_len // tm
    row = lambda i: (i, 0)
    const = lambda i: (0, 0)
    stacked = lambda i: (layer, 0, 0)
    return pl.pallas_call(
        functools.partial(_merge_kernel, final_norm=final_norm),
        grid=(m // tm,),
        in_specs=[pl.BlockSpec((tm, d), row),
                  pl.BlockSpec((tm, ATT_W), row),
                  pl.BlockSpec((tm, D_SSM), row),
                  pl.BlockSpec((tm, d), row),
                  pl.BlockSpec((tm, d), row),
                  pl.BlockSpec((1, 1, d), lambda i: (i // nbl, 0, 0)),
                  pl.BlockSpec((None, ATT_W, d), stacked),
                  pl.BlockSpec((None, D_SSM, d), stacked),
                  pl.BlockSpec((None, d, d), stacked),
                  pl.BlockSpec((1, d), const)],
        out_specs=pl.BlockSpec((tm, d), row),
        out_shape=jax.ShapeDtypeStruct((m, d), F32),
        compiler_params=pltpu.CompilerParams(vmem_limit_bytes=VMEM_LIMIT),
        name="merge_out_projection",
    )(x2, oa, y, gla, gls, gate, w_a, w_s, w_o, final_g)


def _relayout_w_in(w_in):
    w_in = w_in.astype(BF16)
    offs = np.cumsum((0,) + IN_SIZES)
    seg = [w_in[..., int(offs[i]):int(offs[i + 1])] for i in range(len(IN_SIZES))]
    q, k, v, g_att, qi, ki, wi, z, xbc, dt, gla, gls = seg
    zeros = lambda n: jnp.zeros(w_in.shape[:-1] + (n,), w_in.dtype)
    cols = [q, k, k, ki, ki, qi,
            v, v,
            g_att, dt, wi, zeros(MISC_W - N_SSM_HEADS - N_IDX_HEADS), z, xbc, gla, gls]
    return jnp.concatenate(cols, axis=-1)


def _rope_tables(seq_len):
    inv = ROPE_THETA ** (-jnp.arange(0, HEAD_DIM, 2, dtype=F32) / HEAD_DIM)
    ang = jnp.arange(seq_len, dtype=F32)[:, None] * inv[None, :]
    cos, sin = jnp.cos(ang), jnp.sin(ang)
    return (jnp.concatenate([cos, cos, cos, cos], axis=-1),
            jnp.concatenate([-sin, sin, -sin, sin], axis=-1))


def kernel(x, c, w_ada, b_ada, norm_g, w_in, conv_w, conv_b, dt_bias, a_log, d_skip, ssm_norm_g,
           w_branch_a, w_branch_s, w_out, final_g):
    bsz, seq_len, d = x.shape
    depth = w_in.shape[0]
    assert d == D_MODEL and seq_len % CHUNK == 0
    mod = _modulation(c, w_ada, b_ada)
    w_r = _relayout_w_in(w_in)
    w_a = w_branch_a.astype(BF16)
    w_s = w_branch_s.astype(BF16)
    w_o = w_out.astype(BF16)
    cos_t, sin_t = _rope_tables(seq_len)
    expand = (jnp.arange(LANES)[:, None] == (jnp.arange(D_SSM) // SSM_HEAD_DIM)[None, :]).astype(BF16)
    fg = final_g.reshape(1, d)

    x2 = x.reshape(bsz * seq_len, d)
    for i in range(depth):
        shift = mod[i, :, 0:d].reshape(bsz, 1, d)
        scale = mod[i, :, d:2 * d].reshape(bsz, 1, d)
        gate = mod[i, :, 2 * d:].reshape(bsz, 1, d)
        (q, kd, kid, qi, vd, gatt, misc, z, xbc, gla, gls) = _in_projection(
            x2, scale, shift, norm_g[i].reshape(1, d), w_r, i, cos_t, sin_t, seq_len)
        oa = _attention(q, qi, misc, gatt, kd, kid, vd, bsz, seq_len)
        y = _ssd(xbc, misc, z, conv_w[i], conv_b[i], dt_bias[i], a_log[i], d_skip[i], ssm_norm_g[i],
                 expand, bsz, seq_len)
        x2 = _merge(x2, oa, y, gla, gls, gate, w_a, w_s, w_o, i, fg, seq_len, i == depth - 1)
    return x2.reshape(bsz, seq_len, d)
```

```python
import functools
import math

import numpy as np
import jax
import jax.numpy as jnp
from jax import lax
from jax.experimental import pallas as pl
from jax.experimental.pallas import tpu as pltpu

D_MODEL = 1024
N_ATT_HEADS = 8
HEAD_DIM = 64
ATT_W = N_ATT_HEADS * HEAD_DIM
N_IDX_HEADS = 4
IDX_DIM = HEAD_DIM
INDEX_TOPK_MAX = 256
D_SSM = D_MODEL
SSM_HEAD_DIM = 64
N_SSM_HEADS = D_SSM // SSM_HEAD_DIM
N_GROUPS = 2
D_STATE = 128
CONV_K = 4
CHUNK = 128
CONV_CH = D_SSM + 2 * N_GROUPS * D_STATE
ROPE_THETA = 10000.0
EPS = 1e-6
NEG = -1e30
IN_SIZES = (ATT_W, HEAD_DIM, HEAD_DIM, ATT_W, N_IDX_HEADS * IDX_DIM, IDX_DIM, N_IDX_HEADS,
            D_SSM, CONV_CH, N_SSM_HEADS, D_MODEL, D_MODEL)

LANES = 128
ROPE_W = 1024
MISC_W = LANES
WI_OFF = N_SSM_HEADS
LOG2E = math.log2(math.e)
N_PROJ = ROPE_W + LANES + ATT_W + MISC_W + D_SSM + CONV_CH + 2 * D_MODEL
VMEM_LIMIT = 56 * 1024 * 1024
INT_MIN = -2 ** 31
NEG_UP = float(np.nextafter(np.float32(NEG), np.float32(0.0)))

F32 = jnp.float32
BF16 = jnp.bfloat16
_NT = (((1,), (1,)), ((), ()))


def _silu(v):
    return v * jax.nn.sigmoid(v)


def _dot(a, b):
    return jnp.dot(a, b, preferred_element_type=F32)


def _dot_exact(a, b):
    return jnp.dot(a, b, preferred_element_type=F32, precision=lax.Precision.HIGHEST)


def _split3(x):
    hi = x.astype(BF16)
    rest = x - hi.astype(F32)
    mid = rest.astype(BF16)
    return hi, mid, (rest - mid.astype(F32)).astype(BF16)


def _dot_sel(a, b):
    if a.dtype == BF16:
        return sum(_dot(a, piece) for piece in _split3(b))
    return sum(_dot(piece, b) for piece in _split3(a))


def _mod_kernel(c_ref, w_ref, b_ref, o_ref):
    o_ref[0] = _dot_exact(_silu(c_ref[...]), w_ref[0]) + b_ref[0]


def _modulation(c, w_ada, b_ada):
    depth, d, n = w_ada.shape
    bsz = c.shape[0]
    tn = 1024
    return pl.pallas_call(
        _mod_kernel,
        grid=(depth, n // tn),
        in_specs=[pl.BlockSpec((bsz, d), lambda i, j: (0, 0)),
                  pl.BlockSpec((1, d, tn), lambda i, j: (i, 0, j)),
                  pl.BlockSpec((1, 1, tn), lambda i, j: (i, 0, j))],
        out_specs=pl.BlockSpec((1, bsz, tn), lambda i, j: (i, 0, j)),
        out_shape=jax.ShapeDtypeStruct((depth, bsz, n), F32),
        compiler_params=pltpu.CompilerParams(vmem_limit_bytes=VMEM_LIMIT),
        name="adaln_modulation",
    )(c, w_ada, b_ada.reshape(depth, 1, n))


def _inproj_kernel(x_ref, scale_ref, shift_ref, g_ref, w_ref, cos_ref, sin_ref,
                   q_ref, kd_ref, kid_ref, qi_ref, vd_ref, gatt_ref, misc_ref,
                   z_ref, xbc_ref, gla_ref, gls_ref):
    x = x_ref[...]
    h = x * lax.rsqrt(jnp.mean(x * x, axis=-1, keepdims=True) + EPS) * g_ref[...]
    h = h * (1.0 + scale_ref[0]) + shift_ref[0]
    hb = h.astype(BF16)

    def proj(a, b):
        return _dot(hb, w_ref[:, a:b])

    r = proj(0, ROPE_W)
    cos = cos_ref[...]
    sin = sin_ref[...]
    lane = lax.broadcasted_iota(jnp.int32, cos.shape, 1)
    first_half = (lane % HEAD_DIM) < (HEAD_DIM // 2)
    roped = []
    for gi in range(ROPE_W // LANES):
        u = r[:, gi * LANES:(gi + 1) * LANES]
        rot = jnp.where(first_half, pltpu.roll(u, LANES - HEAD_DIM // 2, 1), pltpu.roll(u, HEAD_DIM // 2, 1))
        roped.append(u * cos + rot * sin)
    q_ref[...] = (jnp.concatenate(roped[0:4], axis=1) * (HEAD_DIM ** -0.5 * LOG2E)).astype(BF16)
    kd_ref[...] = roped[4].astype(BF16)
    kid_ref[...] = roped[5].astype(BF16)
    qi_ref[...] = (jnp.concatenate(roped[6:8], axis=1) * IDX_DIM ** -0.5).astype(BF16)

    o = ROPE_W
    vd_ref[...] = proj(o, o + LANES).astype(BF16)
    o += LANES
    gatt_ref[...] = proj(o, o + ATT_W)
    o += ATT_W
    misc_ref[...] = proj(o, o + MISC_W)
    o += MISC_W
    z_ref[...] = proj(o, o + D_SSM)
    o += D_SSM
    xbc_ref[...] = proj(o, o + CONV_CH)
    o += CONV_CH
    gla_ref[...] = proj(o, o + D_MODEL)
    o += D_MODEL
    gls_ref[...] = proj(o, o + D_MODEL)


def _in_projection(x2, scale, shift, norm_g, w_r, layer, cos_t, sin_t, seq_len):
    m, d = x2.shape
    tm = 256
    nbl = seq_len // tm
    row = lambda i: (i, 0)
    widths = [(ATT_W, BF16), (LANES, BF16), (LANES, BF16), (N_IDX_HEADS * IDX_DIM, BF16),
              (LANES, BF16), (ATT_W, F32), (MISC_W, F32),
              (D_SSM, F32), (CONV_CH, F32), (D_MODEL, F32), (D_MODEL, F32)]
    return pl.pallas_call(
        _inproj_kernel,
        grid=(m // tm,),
        in_specs=[pl.BlockSpec((tm, d), row),
                  pl.BlockSpec((1, 1, d), lambda i: (i // nbl, 0, 0)),
                  pl.BlockSpec((1, 1, d), lambda i: (i // nbl, 0, 0)),
                  pl.BlockSpec((1, d), lambda i: (0, 0)),
                  pl.BlockSpec((None, d, N_PROJ), lambda i: (layer, 0, 0)),
                  pl.BlockSpec((tm, LANES), lambda i: (i % nbl, 0)),
                  pl.BlockSpec((tm, LANES), lambda i: (i % nbl, 0))],
        out_specs=[pl.BlockSpec((tm, w), row) for w, _ in widths],
        out_shape=[jax.ShapeDtypeStruct((m, w), dt) for w, dt in widths],
        compiler_params=pltpu.CompilerParams(vmem_limit_bytes=VMEM_LIMIT),
        name="in_projection",
    )(x2, scale, shift, norm_g, w_r, cos_t, sin_t)


def _attn_kernel(q_ref, qi_ref, misc_ref, gatt_ref, kd_ref, kid_ref, vd_ref, tri_ref, o_ref,
                 sc_ref, qs_ref, qis_ref, sa_ref, sb_ref, p_ref, m_ref, l_ref, acc_ref,
                 *, qb, kc_w, topk):
    i = pl.program_id(1)
    q0 = i * qb
    nkc = lax.div(q0 + qb + kc_w - 1, kc_w)
    lane = lax.broadcasted_iota(jnp.int32, (qb, LANES), 1)
    lo = lane < HEAD_DIM
    kf = float(topk)
    n_lg = kc_w // LANES
    rb = 64

    def stack_heads(dst_ref, src_ref, n_heads):
        for h in range(n_heads):
            grp = src_ref[:, (h // 2) * LANES:(h // 2 + 1) * LANES].astype(F32)
            keep = lo if h % 2 == 0 else jnp.logical_not(lo)
            dst_ref[h * qb:(h + 1) * qb, :] = jnp.where(keep, grp, 0.0).astype(BF16)

    def chunk_start(kc):
        return pl.multiple_of(kc * kc_w, kc_w)

    stack_heads(qis_ref, qi_ref, N_IDX_HEADS)
    n_blk = qb // LANES
    blk_lane = lax.broadcasted_iota(jnp.int32, (1, LANES), 1)
    w_rows = []
    for jb in range(n_blk):
        misc_t = misc_ref[jb * LANES:(jb + 1) * LANES, :].T
        w_rows.append([misc_t[WI_OFF + h:WI_OFF + h + 1, :] * N_IDX_HEADS ** -0.5 for h in range(N_IDX_HEADS)])

    def idx_body(kc, masked):
        k0 = chunk_start(kc)
        d = lax.dot_general(kid_ref[pl.ds(k0, kc_w), :], qis_ref[...], _NT, preferred_element_type=F32)
        for jb in range(n_blk):
            s = None
            for h in range(N_IDX_HEADS):
                c0 = h * qb + jb * LANES
                term = jnp.maximum(d[:, c0:c0 + LANES], 0.0) * w_rows[jb][h]
                s = term if s is None else s + term
            if masked:
                kpos = k0 + lax.broadcasted_iota(jnp.int32, (kc_w, 1), 0)
                s = jnp.where(kpos <= q0 + jb * LANES + blk_lane, s, NEG)
            sc_ref[jb, pl.ds(k0, kc_w), :] = s

    def idx_full(kc, carry):
        idx_body(kc, False)
        return carry

    lax.fori_loop(0, nkc - 1, idx_full, 0)
    idx_body(nkc - 1, True)

    assert kc_w == qb
    ncols = jnp.concatenate([jnp.full((1, LANES), (jb + 1) * LANES, jnp.int32) for jb in range(n_blk)], axis=0)
    ncols = (ncols + (nkc - 1) * kc_w).astype(F32)

    def count_below(cand):
        cand_b = [jnp.broadcast_to(cand[jb:jb + 1, :], (rb, LANES)) for jb in range(n_blk)]

        def below(jb, k0, r, acc):
            diff = sc_ref[jb, pl.ds(k0 + r * rb, rb), :] - cand_b[jb]
            return acc + lax.shift_right_logical(pltpu.bitcast(diff, jnp.int32), 31)

        def body(kc, accs):
            k0 = chunk_start(kc)
            out = []
            for jb in range(n_blk):
                acc = accs[jb]
                for r in range(kc_w // rb):
                    acc = below(jb, k0, r, acc)
                out.append(acc)
            return tuple(out)

        accs = lax.fori_loop(0, nkc - 1, body, tuple(jnp.zeros((rb, LANES), jnp.int32) for _ in range(n_blk)))
        k0 = chunk_start(nkc - 1)
        accs = list(accs)
        for jb in range(n_blk):
            for r in range((jb + 1) * LANES // rb):
                accs[jb] = below(jb, k0, r, accs[jb])
        return jnp.concatenate([jnp.sum(a.astype(F32), axis=0, keepdims=True) for a in accs], axis=0)

    def ordinal_to_float(c):
        bits = jnp.where(c > 0, c, jnp.int32(INT_MIN) - c)
        return pltpu.bitcast(bits, F32)

    cnt0 = ncols - count_below(jnp.full((n_blk, LANES), -0.0, F32))
    nonneg = cnt0 >= kf
    prefix0 = jnp.where(nonneg, 0, jnp.int32(INT_MIN))
    cnt_at0 = jnp.where(nonneg, cnt0, ncols)

    def bit_body(b, carry):
        prefix, cnt_at = carry
        cand = prefix | lax.shift_left(jnp.int32(1), 30 - b)
        cnt = ncols - count_below(ordinal_to_float(cand))
        ok = cnt >= kf
        return jnp.where(ok, cand, prefix), jnp.where(ok, cnt, cnt_at)

    prefix, cnt_at = lax.fori_loop(0, 31, bit_body, (prefix0, cnt_at0))
    tiny = jnp.logical_and(prefix > -(1 << 23), prefix < (1 << 23))
    thr = jnp.where(prefix == jnp.int32(INT_MIN), -jnp.inf,
                    jnp.where(tiny, 0.0, ordinal_to_float(prefix)))

    excess = cnt_at - kf
    tied = jnp.logical_and(excess > 0.0, thr > NEG)
    any_tied = jnp.max(jnp.where(tied, 1.0, 0.0)) > 0.5

    @pl.when(any_tied)
    def _():
        def drop(i, seen):
            k0 = chunk_start(nkc - 1 - i)
            out = []
            for jb in range(n_blk):
                after = seen[jb]
                for t in reversed(range(n_lg)):
                    ch = sc_ref[jb, pl.ds(k0 + t * LANES, LANES), :]
                    tie = ch == thr[jb:jb + 1, :]
                    rank = after + _dot(tri_ref[...], jnp.where(tie, 1.0, 0.0).astype(BF16))
                    dropped = jnp.logical_and(tie, rank <= excess[jb:jb + 1, :])
                    sc_ref[jb, pl.ds(k0 + t * LANES, LANES), :] = jnp.where(dropped, NEG, ch)
                    after = rank[0:1, :]
                out.append(after)
            return tuple(out)

        lax.fori_loop(0, nkc, drop, tuple(jnp.zeros((1, LANES), F32) for _ in range(n_blk)))

    thr = jnp.maximum(thr, NEG_UP)

    m_ref[...] = jnp.full(m_ref.shape, NEG, F32)
    l_ref[...] = jnp.zeros(l_ref.shape, F32)
    acc_ref[...] = jnp.zeros(acc_ref.shape, F32)
    stack_heads(qs_ref, q_ref, N_ATT_HEADS)

    def logits(kc, s_ref):
        k0 = chunk_start(kc)
        bias = []
        for t in range(n_lg):
            blocks = [jnp.where(sc_ref[jb, pl.ds(k0 + t * LANES, LANES), :] >= thr[jb:jb + 1, :], 0.0, NEG).T
                      for jb in range(n_blk)]
            bias.append(jnp.concatenate(blocks, axis=0))
        d = lax.dot_general(qs_ref[...], kd_ref[pl.ds(k0, kc_w), :], _NT, preferred_element_type=F32)
        for h in range(N_ATT_HEADS):
            for t in range(n_lg):
                s_ref[h * qb:(h + 1) * qb, t * LANES:(t + 1) * LANES] = (
                    d[h * qb:(h + 1) * qb, t * LANES:(t + 1) * LANES] + bias[t])

    def softmax_pv(kc, s_ref):
        k0 = chunk_start(kc)
        alphas = []
        for h in range(N_ATT_HEADS):
            rows = slice(h * qb, (h + 1) * qb)
            masked = lambda t: s_ref[rows, t * LANES:(t + 1) * LANES]
            mx = masked(0)
            for t in range(1, n_lg):
                mx = jnp.maximum(mx, masked(t))
            m_old = m_ref[h]
            m_new = jnp.maximum(m_old, jnp.max(mx, axis=-1, keepdims=True))
            alpha = jnp.exp2(m_old - m_new)
            m_ref[h] = m_new
            psum = None
            for t in range(n_lg):
                p = jnp.exp2(masked(t) - m_new)
                p_ref[rows, t * LANES:(t + 1) * LANES] = p.astype(BF16)
                psum = p if psum is None else psum + p
            l_ref[h] = alpha * l_ref[h] + psum
            alphas.append(alpha)
        pv = _dot(p_ref[...], vd_ref[pl.ds(k0, kc_w), :])
        for g in range(N_ATT_HEADS // 2):
            pair = jnp.where(lo, pv[2 * g * qb:(2 * g + 1) * qb], pv[(2 * g + 1) * qb:(2 * g + 2) * qb])
            acc_ref[g] = acc_ref[g] * jnp.where(lo, alphas[2 * g], alphas[2 * g + 1]) + pair

    logits(0, sa_ref)

    def att_pair(j, carry):
        kc = 2 * j
        logits(kc + 1, sb_ref)
        softmax_pv(kc, sa_ref)
        logits(jnp.minimum(kc + 2, nkc - 1), sa_ref)
        softmax_pv(kc + 1, sb_ref)
        return carry

    lax.fori_loop(0, lax.div(nkc, 2), att_pair, 0)

    @pl.when(lax.rem(nkc, 2) == 1)
    def _():
        softmax_pv(nkc - 1, sa_ref)

    outs = []
    for g in range(N_ATT_HEADS // 2):
        l_lo = jnp.sum(l_ref[2 * g], axis=-1, keepdims=True)
        l_hi = jnp.sum(l_ref[2 * g + 1], axis=-1, keepdims=True)
        outs.append(acc_ref[g] * jnp.where(lo, 1.0 / l_lo, 1.0 / l_hi))
    o_ref[...] = (jnp.concatenate(outs, axis=1) * _silu(gatt_ref[...])).astype(BF16)


def _attention(q, qi, misc, gatt, kd, kid, vd, bsz, seq_len):
    m = q.shape[0]
    qb = min(512, seq_len)
    kc_w = min(512, seq_len)
    topk = min(INDEX_TOPK_MAX, seq_len // 4)
    assert seq_len % kc_w == 0 and kc_w == qb and qb % LANES == 0 and kc_w >= topk
    nqb = seq_len // qb
    qrow = lambda b, i: (b * nqb + i, 0)
    kv = lambda b, i: (b, 0)
    kern = functools.partial(_attn_kernel, qb=qb, kc_w=kc_w, topk=topk)
    tri = (jnp.arange(LANES)[:, None] <= jnp.arange(LANES)[None, :]).astype(BF16)
    return pl.pallas_call(
        kern,
        grid=(bsz, nqb),
        in_specs=[pl.BlockSpec((qb, ATT_W), qrow),
                  pl.BlockSpec((qb, N_IDX_HEADS * IDX_DIM), qrow),
                  pl.BlockSpec((qb, MISC_W), qrow),
                  pl.BlockSpec((qb, ATT_W), qrow),
                  pl.BlockSpec((seq_len, LANES), kv),
                  pl.BlockSpec((seq_len, LANES), kv),
                  pl.BlockSpec((seq_len, LANES), kv),
                  pl.BlockSpec((LANES, LANES), lambda b, i: (0, 0))],
        out_specs=pl.BlockSpec((qb, ATT_W), qrow),
        out_shape=jax.ShapeDtypeStruct((m, ATT_W), BF16),
        scratch_shapes=[pltpu.VMEM((qb // LANES, seq_len, LANES), F32),
                        pltpu.VMEM((N_ATT_HEADS * qb, LANES), BF16),
                        pltpu.VMEM((N_IDX_HEADS * qb, LANES), BF16),
                        pltpu.VMEM((N_ATT_HEADS * qb, kc_w), F32),
                        pltpu.VMEM((N_ATT_HEADS * qb, kc_w), F32),
                        pltpu.VMEM((N_ATT_HEADS * qb, kc_w), BF16),
                        pltpu.VMEM((N_ATT_HEADS, qb, LANES), F32),
                        pltpu.VMEM((N_ATT_HEADS, qb, LANES), F32),
                        pltpu.VMEM((N_ATT_HEADS // 2, qb, LANES), F32)],
        compiler_params=pltpu.CompilerParams(vmem_limit_bytes=VMEM_LIMIT,
                                             dimension_semantics=("arbitrary", "arbitrary")),
        name="dsa_attention",
    )(q, qi, misc, gatt, kd, kid, vd, tri)


def _ssd_kernel(xbc_ref, misc_ref, z_ref, cw_ref, cb_ref, dtb_ref, alog_ref, dsk_ref, ng_ref, e_ref,
                y_ref, ext_ref, st_ref):
    c = pl.program_id(1)
    gw = D_SSM // N_GROUPS
    tail = 8

    @pl.when(c == 0)
    def _():
        ext_ref[0:tail, :] = jnp.zeros((tail, CONV_CH), F32)
        st_ref[...] = jnp.zeros(st_ref.shape, F32)

    ext_ref[tail:tail + CHUNK, :] = xbc_ref[...]
    conv = cb_ref[...]
    for j in range(CONV_K):
        conv = conv + ext_ref[tail - j:tail - j + CHUNK, :] * cw_ref[CONV_K - 1 - j:CONV_K - j, :]
    ext_ref[0:tail, :] = ext_ref[CHUNK:CHUNK + tail, :]
    u = _silu(conv)
    xs = u[:, 0:D_SSM]
    bm = u[:, D_SSM:D_SSM + N_GROUPS * D_STATE]
    cm = u[:, D_SSM + N_GROUPS * D_STATE:]

    row = lax.broadcasted_iota(jnp.int32, (CHUNK, LANES), 0)
    lane = lax.broadcasted_iota(jnp.int32, (CHUNK, LANES), 1)
    tril = row >= lane
    lo = lane < SSM_HEAD_DIM

    dtr = misc_ref[...] + dtb_ref[...]
    softplus = jnp.maximum(dtr, 0.0) + jnp.log1p(jnp.exp(-jnp.abs(dtr)))
    dtv = jnp.where(lane < N_SSM_HEADS, softplus, 0.0)
    adt = dtv * (-jnp.exp(alog_ref[...]))
    acs = _dot_sel(jnp.where(tril, 1.0, 0.0).astype(BF16), adt)
    acs_t = acs.T
    expand = e_ref[...]
    dt_e = _dot_sel(dtv, expand)
    acs_e = _dot_sel(acs, expand)
    last_e = acs_e[CHUNK - 1:CHUNK, :]
    xd = xs * dt_e
    w_state = (xd * jnp.exp(last_e - acs_e)).astype(BF16)
    decay_in = jnp.exp(acs_e)
    chunk_decay = jnp.exp(last_e)

    ys = []
    for g in range(N_GROUPS):
        bm_g = bm[:, g * D_STATE:(g + 1) * D_STATE]
        cmb = cm[:, g * D_STATE:(g + 1) * D_STATE].astype(BF16)
        cb = lax.dot_general(cmb, bm_g.astype(BF16), _NT, preferred_element_type=F32)
        st = st_ref[g]
        y_off = _dot(cmb, st.astype(BF16)) * decay_in[:, g * gw:(g + 1) * gw]
        st_ref[g] = st * chunk_decay[:, g * gw:(g + 1) * gw] + _dot(bm_g.T.astype(BF16), w_state[:, g * gw:(g + 1) * gw])
        for p in range(gw // LANES):
            pair = g * (gw // LANES) + p
            xp = xd[:, pair * LANES:(pair + 1) * LANES]
            acc = y_off[:, p * LANES:(p + 1) * LANES]
            for half in range(2):
                h = 2 * pair + half
                seg = acs[:, h:h + 1] - acs_t[h:h + 1, :]
                decay = jnp.exp(jnp.where(tril, seg, -jnp.inf))
                xh = jnp.where(lo if half == 0 else jnp.logical_not(lo), xp, 0.0).astype(BF16)
                acc = acc + _dot((cb * decay).astype(BF16), xh)
            ys.append(acc)
    y = jnp.concatenate(ys, axis=1) + dsk_ref[...] * xs
    y = y * _silu(z_ref[...])
    normed = []
    for g in range(N_GROUPS):
        yg = y[:, g * gw:(g + 1) * gw]
        normed.append(yg * lax.rsqrt(jnp.mean(yg * yg, axis=-1, keepdims=True) + EPS))
    y_ref[...] = (jnp.concatenate(normed, axis=1) * ng_ref[...]).astype(BF16)


def _ssd(xbc, misc, z, conv_w, conv_b, dt_bias, a_log, d_skip, ssm_norm_g, expand, bsz, seq_len):
    m = xbc.shape[0]
    nc = seq_len // CHUNK
    rowc = lambda b, c: (b * nc + c, 0)
    const = lambda b, c: (0, 0)
    pad = lambda v: jnp.pad(v, (0, LANES - v.shape[0])).reshape(1, LANES)
    return pl.pallas_call(
        _ssd_kernel,
        grid=(bsz, nc),
        in_specs=[pl.BlockSpec((CHUNK, CONV_CH), rowc),
                  pl.BlockSpec((CHUNK, MISC_W), rowc),
                  pl.BlockSpec((CHUNK, D_SSM), rowc),
                  pl.BlockSpec((CONV_K, CONV_CH), const),
                  pl.BlockSpec((1, CONV_CH), const),
                  pl.BlockSpec((1, LANES), const),
                  pl.BlockSpec((1, LANES), const),
                  pl.BlockSpec((1, D_SSM), const),
                  pl.BlockSpec((1, D_SSM), const),
                  pl.BlockSpec((LANES, D_SSM), const)],
        out_specs=pl.BlockSpec((CHUNK, D_SSM), rowc),
        out_shape=jax.ShapeDtypeStruct((m, D_SSM), BF16),
        scratch_shapes=[pltpu.VMEM((CHUNK + 8, CONV_CH), F32),
                        pltpu.VMEM((N_GROUPS, D_STATE, D_SSM // N_GROUPS), F32)],
        compiler_params=pltpu.CompilerParams(vmem_limit_bytes=VMEM_LIMIT,
                                             dimension_semantics=("arbitrary", "arbitrary")),
        name="ssd_mixer",
    )(xbc, misc, z, conv_w, conv_b.reshape(1, CONV_CH), pad(dt_bias), pad(a_log),
      jnp.repeat(d_skip, SSM_HEAD_DIM).reshape(1, D_SSM), ssm_norm_g.reshape(1, D_SSM), expand)


def _merge_kernel(x_ref, oa_ref, y_ref, gla_ref, gls_ref, gate_ref, wa_ref, ws_ref, wo_ref, fg_ref, o_ref,
                  *, final_norm):
    y_att = _dot(oa_ref[...], wa_ref[...])
    y_ssd = _dot(y_ref[...], ws_ref[...])
    merged = jax.nn.sigmoid(gla_ref[...]) * y_att + jax.nn.sigmoid(gls_ref[...]) * y_ssd
    out = x_ref[...] + gate_ref[0] * _dot(merged.astype(BF16), wo_ref[...])
    if final_norm:
        out = out * lax.rsqrt(jnp.mean(out * out, axis=-1, keepdims=True) + EPS) * fg_ref[...]
    o_ref[...] = out


def _merge(x2, oa, y, gla, gls, gate, w_a, w_s, w_o, layer, final_g, seq_len, final_norm):
    m, d = x2.shape
    tm = 512
    nbl = seq_len // tm
    row = lambda i: (i, 0)
    const = lambda i: (0, 0)
    stacked = lambda i: (layer, 0, 0)
    return pl.pallas_call(
        functools.partial(_merge_kernel, final_norm=final_norm),
        grid=(m // tm,),
        in_specs=[pl.BlockSpec((tm, d), row),
                  pl.BlockSpec((tm, ATT_W), row),
                  pl.BlockSpec((tm, D_SSM), row),
                  pl.BlockSpec((tm, d), row),
                  pl.BlockSpec((tm, d), row),
                  pl.BlockSpec((1, 1, d), lambda i: (i // nbl, 0, 0)),
                  pl.BlockSpec((None, ATT_W, d), stacked),
                  pl.BlockSpec((None, D_SSM, d), stacked),
                  pl.BlockSpec((None, d, d), stacked),
                  pl.BlockSpec((1, d), const)],
        out_specs=pl.BlockSpec((tm, d), row),
        out_shape=jax.ShapeDtypeStruct((m, d), F32),
        compiler_params=pltpu.CompilerParams(vmem_limit_bytes=VMEM_LIMIT),
        name="merge_out_projection",
    )(x2, oa, y, gla, gls, gate, w_a, w_s, w_o, final_g)


def _relayout_w_in(w_in):
    w_in = w_in.astype(BF16)
    offs = np.cumsum((0,) + IN_SIZES)
    seg = [w_in[..., int(offs[i]):int(offs[i + 1])] for i in range(len(IN_SIZES))]
    q, k, v, g_att, qi, ki, wi, z, xbc, dt, gla, gls = seg
    zeros = lambda n: jnp.zeros(w_in.shape[:-1] + (n,), w_in.dtype)
    cols = [q, k, k, ki, ki, qi,
            v, v,
            g_att, dt, wi, zeros(MISC_W - N_SSM_HEADS - N_IDX_HEADS), z, xbc, gla, gls]
    return jnp.concatenate(cols, axis=-1)


def _rope_tables(seq_len):
    inv = ROPE_THETA ** (-jnp.arange(0, HEAD_DIM, 2, dtype=F32) / HEAD_DIM)
    ang = jnp.arange(seq_len, dtype=F32)[:, None] * inv[None, :]
    cos, sin = jnp.cos(ang), jnp.sin(ang)
    return (jnp.concatenate([cos, cos, cos, cos], axis=-1),
            jnp.concatenate([-sin, sin, -sin, sin], axis=-1))


def kernel(x, c, w_ada, b_ada, norm_g, w_in, conv_w, conv_b, dt_bias, a_log, d_skip, ssm_norm_g,
           w_branch_a, w_branch_s, w_out, final_g):
    bsz, seq_len, d = x.shape
    depth = w_in.shape[0]
    assert d == D_MODEL and seq_len % CHUNK == 0
    mod = _modulation(c, w_ada, b_ada)
    w_r = _relayout_w_in(w_in)
    w_a = w_branch_a.astype(BF16)
    w_s = w_branch_s.astype(BF16)
    w_o = w_out.astype(BF16)
    cos_t, sin_t = _rope_tables(seq_len)
    expand = (jnp.arange(LANES)[:, None] == (jnp.arange(D_SSM) // SSM_HEAD_DIM)[None, :]).astype(BF16)
    fg = final_g.reshape(1, d)

    x2 = x.reshape(bsz * seq_len, d)
    for i in range(depth):
        shift = mod[i, :, 0:d].reshape(bsz, 1, d)
        scale = mod[i, :, d:2 * d].reshape(bsz, 1, d)
        gate = mod[i, :, 2 * d:].reshape(bsz, 1, d)
        (q, kd, kid, qi, vd, gatt, misc, z, xbc, gla, gls) = _in_projection(
            x2, scale, shift, norm_g[i].reshape(1, d), w_r, i, cos_t, sin_t, seq_len)
        oa = _attention(q, qi, misc, gatt, kd, kid, vd, bsz, seq_len)
        y = _ssd(xbc, misc, z, conv_w[i], conv_b[i], dt_bias[i], a_log[i], d_skip[i], ssm_norm_g[i],
                 expand, bsz, seq_len)
        x2 = _merge(x2, oa, y, gla, gls, gate, w_a, w_s, w_o, i, fg, seq_len, i == depth - 1)
    return x2.reshape(bsz, seq_len, d)
```

```python
import functools
import math

import numpy as np
import jax
import jax.numpy as jnp
from jax import lax
from jax.experimental import pallas as pl
from jax.experimental.pallas import tpu as pltpu

D_MODEL = 1024
N_ATT_HEADS = 8
HEAD_DIM = 64
ATT_W = N_ATT_HEADS * HEAD_DIM
N_IDX_HEADS = 4
IDX_DIM = HEAD_DIM
INDEX_TOPK_MAX = 256
D_SSM = D_MODEL
SSM_HEAD_DIM = 64
N_SSM_HEADS = D_SSM // SSM_HEAD_DIM
N_GROUPS = 2
D_STATE = 128
CONV_K = 4
CHUNK = 128
CONV_CH = D_SSM + 2 * N_GROUPS * D_STATE
ROPE_THETA = 10000.0
EPS = 1e-6
NEG = -1e30
IN_SIZES = (ATT_W, HEAD_DIM, HEAD_DIM, ATT_W, N_IDX_HEADS * IDX_DIM, IDX_DIM, N_IDX_HEADS,
            D_SSM, CONV_CH, N_SSM_HEADS, D_MODEL, D_MODEL)

LANES = 128
ROPE_W = 1024
MISC_W = LANES
WI_OFF = N_SSM_HEADS
LOG2E = math.log2(math.e)
N_PROJ = ROPE_W + LANES + ATT_W + MISC_W + D_SSM + CONV_CH + 2 * D_MODEL
VMEM_LIMIT = 56 * 1024 * 1024
INT_MIN = -2 ** 31
NEG_UP = float(np.nextafter(np.float32(NEG), np.float32(0.0)))

F32 = jnp.float32
BF16 = jnp.bfloat16
_NT = (((1,), (1,)), ((), ()))


def _silu(v):
    return v * jax.nn.sigmoid(v)


def _dot(a, b):
    return jnp.dot(a, b, preferred_element_type=F32)


def _dot_exact(a, b):
    return jnp.dot(a, b, preferred_element_type=F32, precision=lax.Precision.HIGHEST)


def _split3(x):
    hi = x.astype(BF16)
    rest = x - hi.astype(F32)
    mid = rest.astype(BF16)
    return hi, mid, (rest - mid.astype(F32)).astype(BF16)


def _dot_sel(a, b):
    if a.dtype == BF16:
        return sum(_dot(a, piece) for piece in _split3(b))
    return sum(_dot(piece, b) for piece in _split3(a))


def _mod_kernel(c_ref, w_ref, b_ref, o_ref):
    o_ref[0] = _dot_exact(_silu(c_ref[...]), w_ref[0]) + b_ref[0]


def _modulation(c, w_ada, b_ada):
    depth, d, n = w_ada.shape
    bsz = c.shape[0]
    tn = 1024
    return pl.pallas_call(
        _mod_kernel,
        grid=(depth, n // tn),
        in_specs=[pl.BlockSpec((bsz, d), lambda i, j: (0, 0)),
                  pl.BlockSpec((1, d, tn), lambda i, j: (i, 0, j)),
                  pl.BlockSpec((1, 1, tn), lambda i, j: (i, 0, j))],
        out_specs=pl.BlockSpec((1, bsz, tn), lambda i, j: (i, 0, j)),
        out_shape=jax.ShapeDtypeStruct((depth, bsz, n), F32),
        compiler_params=pltpu.CompilerParams(vmem_limit_bytes=VMEM_LIMIT),
        name="adaln_modulation",
    )(c, w_ada, b_ada.reshape(depth, 1, n))


def _inproj_kernel(x_ref, scale_ref, shift_ref, g_ref, w_ref, cos_ref, sin_ref,
                   q_ref, kd_ref, kid_ref, qi_ref, vd_ref, gatt_ref, misc_ref,
                   z_ref, xbc_ref, gla_ref, gls_ref):
    x = x_ref[...]
    h = x * lax.rsqrt(jnp.mean(x * x, axis=-1, keepdims=True) + EPS) * g_ref[...]
    h = h * (1.0 + scale_ref[0]) + shift_ref[0]
    hb = h.astype(BF16)

    def proj(a, b):
        return _dot(hb, w_ref[:, a:b])

    r = proj(0, ROPE_W)
    cos = cos_ref[...]
    sin = sin_ref[...]
    lane = lax.broadcasted_iota(jnp.int32, cos.shape, 1)
    first_half = (lane % HEAD_DIM) < (HEAD_DIM // 2)
    roped = []
    for gi in range(ROPE_W // LANES):
        u = r[:, gi * LANES:(gi + 1) * LANES]
        rot = jnp.where(first_half, pltpu.roll(u, LANES - HEAD_DIM // 2, 1), pltpu.roll(u, HEAD_DIM // 2, 1))
        roped.append(u * cos + rot * sin)
    q_ref[...] = (jnp.concatenate(roped[0:4], axis=1) * (HEAD_DIM ** -0.5 * LOG2E)).astype(BF16)
    kd_ref[...] = roped[4].astype(BF16)
    kid_ref[...] = roped[5].astype(BF16)
    qi_ref[...] = (jnp.concatenate(roped[6:8], axis=1) * IDX_DIM ** -0.5).astype(BF16)

    o = ROPE_W
    vd_ref[...] = proj(o, o + LANES).astype(BF16)
    o += LANES
    gatt_ref[...] = proj(o, o + ATT_W)
    o += ATT_W
    misc_ref[...] = proj(o, o + MISC_W)
    o += MISC_W
    z_ref[...] = proj(o, o + D_SSM)
    o += D_SSM
    xbc_ref[...] = proj(o, o + CONV_CH)
    o += CONV_CH
    gla_ref[...] = proj(o, o + D_MODEL)
    o += D_MODEL
    gls_ref[...] = proj(o, o + D_MODEL)


def _in_projection(x2, scale, shift, norm_g, w_r, layer, cos_t, sin_t, seq_len):
    m, d = x2.shape
    tm = 256
    nbl = seq_len // tm
    row = lambda i: (i, 0)
    widths = [(ATT_W, BF16), (LANES, BF16), (LANES, BF16), (N_IDX_HEADS * IDX_DIM, BF16),
              (LANES, BF16), (ATT_W, F32), (MISC_W, F32),
              (D_SSM, F32), (CONV_CH, F32), (D_MODEL, F32), (D_MODEL, F32)]
    return pl.pallas_call(
        _inproj_kernel,
        grid=(m // tm,),
        in_specs=[pl.BlockSpec((tm, d), row),
                  pl.BlockSpec((1, 1, d), lambda i: (i // nbl, 0, 0)),
                  pl.BlockSpec((1, 1, d), lambda i: (i // nbl, 0, 0)),
                  pl.BlockSpec((1, d), lambda i: (0, 0)),
                  pl.BlockSpec((None, d, N_PROJ), lambda i: (layer, 0, 0)),
                  pl.BlockSpec((tm, LANES), lambda i: (i % nbl, 0)),
                  pl.BlockSpec((tm, LANES), lambda i: (i % nbl, 0))],
        out_specs=[pl.BlockSpec((tm, w), row) for w, _ in widths],
        out_shape=[jax.ShapeDtypeStruct((m, w), dt) for w, dt in widths],
        compiler_params=pltpu.CompilerParams(vmem_limit_bytes=VMEM_LIMIT),
        name="in_projection",
    )(x2, scale, shift, norm_g, w_r, cos_t, sin_t)


def _attn_kernel(q_ref, qi_ref, misc_ref, gatt_ref, kd_ref, kid_ref, vd_ref, tri_ref, o_ref,
                 sc_ref, qs_ref, qis_ref, sa_ref, sb_ref, p_ref, m_ref, l_ref, acc_ref,
                 *, qb, kc_w, topk):
    i = pl.program_id(1)
    q0 = i * qb
    nkc = lax.div(q0 + qb + kc_w - 1, kc_w)
    lane = lax.broadcasted_iota(jnp.int32, (qb, LANES), 1)
    lo = lane < HEAD_DIM
    kf = float(topk)
    n_lg = kc_w // LANES
    rb = 64

    def stack_heads(dst_ref, src_ref, n_heads):
        for h in range(n_heads):
            grp = src_ref[:, (h // 2) * LANES:(h // 2 + 1) * LANES].astype(F32)
            keep = lo if h % 2 == 0 else jnp.logical_not(lo)
            dst_ref[h * qb:(h + 1) * qb, :] = jnp.where(keep, grp, 0.0).astype(BF16)

    def chunk_start(kc):
        return pl.multiple_of(kc * kc_w, kc_w)

    stack_heads(qis_ref, qi_ref, N_IDX_HEADS)
    n_blk = qb // LANES
    blk_lane = lax.broadcasted_iota(jnp.int32, (1, LANES), 1)
    w_rows = []
    for jb in range(n_blk):
        misc_t = misc_ref[jb * LANES:(jb + 1) * LANES, :].T
        w_rows.append([misc_t[WI_OFF + h:WI_OFF + h + 1, :] * N_IDX_HEADS ** -0.5 for h in range(N_IDX_HEADS)])

    def idx_body(kc, masked):
        k0 = chunk_start(kc)
        d = lax.dot_general(kid_ref[pl.ds(k0, kc_w), :], qis_ref[...], _NT, preferred_element_type=F32)
        for jb in range(n_blk):
            s = None
            for h in range(N_IDX_HEADS):
                c0 = h * qb + jb * LANES
                term = jnp.maximum(d[:, c0:c0 + LANES], 0.0) * w_rows[jb][h]
                s = term if s is None else s + term
            if masked:
                kpos = k0 + lax.broadcasted_iota(jnp.int32, (kc_w, 1), 0)
                s = jnp.where(kpos <= q0 + jb * LANES + blk_lane, s, NEG)
            sc_ref[jb, pl.ds(k0, kc_w), :] = s

    def idx_full(kc, carry):
        idx_body(kc, False)
        return carry

    lax.fori_loop(0, nkc - 1, idx_full, 0)
    idx_body(nkc - 1, True)

    assert kc_w == qb
    ncols = jnp.concatenate([jnp.full((1, LANES), (jb + 1) * LANES, jnp.int32) for jb in range(n_blk)], axis=0)
    ncols = (ncols + (nkc - 1) * kc_w).astype(F32)

    def count_below(cand):
        cand_b = [jnp.broadcast_to(cand[jb:jb + 1, :], (rb, LANES)) for jb in range(n_blk)]

        def below(jb, k0, r, acc):
            diff = sc_ref[jb, pl.ds(k0 + r * rb, rb), :] - cand_b[jb]
            return acc + lax.shift_right_logical(pltpu.bitcast(diff, jnp.int32), 31)

        def body(kc, accs):
            k0 = chunk_start(kc)
            out = []
            for jb in range(n_blk):
                acc = accs[jb]
                for r in range(kc_w // rb):
                    acc = below(jb, k0, r, acc)
                out.append(acc)
            return tuple(out)

        accs = lax.fori_loop(0, nkc - 1, body, tuple(jnp.zeros((rb, LANES), jnp.int32) for _ in range(n_blk)))
        k0 = chunk_start(nkc - 1)
        accs = list(accs)
        for jb in range(n_blk):
            for r in range((jb + 1) * LANES // rb):
                accs[jb] = below(jb, k0, r, accs[jb])
        return jnp.concatenate([jnp.sum(a.astype(F32), axis=0, keepdims=True) for a in accs], axis=0)

    def ordinal_to_float(c):
        bits = jnp.where(c > 0, c, jnp.int32(INT_MIN) - c)
        return pltpu.bitcast(bits, F32)

    cnt0 = ncols - count_below(jnp.full((n_blk, LANES), -0.0, F32))
    nonneg = cnt0 >= kf
    prefix0 = jnp.where(nonneg, 0, jnp.int32(INT_MIN))
    cnt_at0 = jnp.where(nonneg, cnt0, ncols)

    def bit_body(b, carry):
        prefix, cnt_at = carry
        cand = prefix | lax.shift_left(jnp.int32(1), 30 - b)
        cnt = ncols - count_below(ordinal_to_float(cand))
        ok = cnt >= kf
        return jnp.where(ok, cand, prefix), jnp.where(ok, cnt, cnt_at)

    prefix, cnt_at = lax.fori_loop(0, 31, bit_body, (prefix0, cnt_at0))
    tiny = jnp.logical_and(prefix > -(1 << 23), prefix < (1 << 23))
    thr = jnp.where(prefix == jnp.int32(INT_MIN), -jnp.inf,
                    jnp.where(tiny, 0.0, ordinal_to_float(prefix)))

    excess = cnt_at - kf
    tied = jnp.logical_and(excess > 0.0, thr > NEG)
    any_tied = jnp.max(jnp.where(tied, 1.0, 0.0)) > 0.5

    @pl.when(any_tied)
    def _():
        def drop(i, seen):
            k0 = chunk_start(nkc - 1 - i)
            out = []
            for jb in range(n_blk):
                after = seen[jb]
                for t in reversed(range(n_lg)):
                    ch = sc_ref[jb, pl.ds(k0 + t * LANES, LANES), :]
                    tie = ch == thr[jb:jb + 1, :]
                    rank = after + _dot(tri_ref[...], jnp.where(tie, 1.0, 0.0).astype(BF16))
                    dropped = jnp.logical_and(tie, rank <= excess[jb:jb + 1, :])
                    sc_ref[jb, pl.ds(k0 + t * LANES, LANES), :] = jnp.where(dropped, NEG, ch)
                    after = rank[0:1, :]
                out.append(after)
            return tuple(out)

        lax.fori_loop(0, nkc, drop, tuple(jnp.zeros((1, LANES), F32) for _ in range(n_blk)))

    thr = jnp.maximum(thr, NEG_UP)

    m_ref[...] = jnp.full(m_ref.shape, NEG, F32)
    l_ref[...] = jnp.zeros(l_ref.shape, F32)
    acc_ref[...] = jnp.zeros(acc_ref.shape, F32)
    stack_heads(qs_ref, q_ref, N_ATT_HEADS)

    def logits(kc, s_ref):
        k0 = chunk_start(kc)
        bias = []
        for t in range(n_lg):
            blocks = [jnp.where(sc_ref[jb, pl.ds(k0 + t * LANES, LANES), :] >= thr[jb:jb + 1, :], 0.0, NEG).T
                      for jb in range(n_blk)]
            bias.append(jnp.concatenate(blocks, axis=0))
        d = lax.dot_general(qs_ref[...], kd_ref[pl.ds(k0, kc_w), :], _NT, preferred_element_type=F32)
        for h in range(N_ATT_HEADS):
            for t in range(n_lg):
                s_ref[h * qb:(h + 1) * qb, t * LANES:(t + 1) * LANES] = (
                    d[h * qb:(h + 1) * qb, t * LANES:(t + 1) * LANES] + bias[t])

    def softmax_pv(kc, s_ref, diagonal=False):
        k0 = chunk_start(kc)
        groups = [(jb * LANES, LANES, jb + 1) for jb in range(n_blk)] if diagonal else [(0, qb, n_lg)]
        alphas = []
        for h in range(N_ATT_HEADS):
            alpha_rows = []
            for off, nrows, nt in groups:
                rows = slice(h * qb + off, h * qb + off + nrows)
                sub = slice(off, off + nrows)
                masked = lambda t: s_ref[rows, t * LANES:(t + 1) * LANES]
                mx = masked(0)
                for t in range(1, nt):
                    mx = jnp.maximum(mx, masked(t))
                m_old = m_ref[h, sub, :]
                m_new = jnp.maximum(m_old, jnp.max(mx, axis=-1, keepdims=True))
                alpha = jnp.exp2(m_old - m_new)
                m_ref[h, sub, :] = m_new
                psum = None
                for t in range(n_lg):
                    if t < nt:
                        p = jnp.exp2(masked(t) - m_new)
                        psum = p if psum is None else psum + p
                        p_ref[rows, t * LANES:(t + 1) * LANES] = p.astype(BF16)
                    else:
                        p_ref[rows, t * LANES:(t + 1) * LANES] = jnp.zeros((nrows, LANES), BF16)
                l_ref[h, sub, :] = alpha * l_ref[h, sub, :] + psum
                alpha_rows.append(alpha)
            alphas.append(alpha_rows[0] if len(alpha_rows) == 1 else jnp.concatenate(alpha_rows, axis=0))
        pv = _dot(p_ref[...], vd_ref[pl.ds(k0, kc_w), :])
        for g in range(N_ATT_HEADS // 2):
            pair = jnp.where(lo, pv[2 * g * qb:(2 * g + 1) * qb], pv[(2 * g + 1) * qb:(2 * g + 2) * qb])
            acc_ref[g] = acc_ref[g] * jnp.where(lo, alphas[2 * g], alphas[2 * g + 1]) + pair

    logits(0, sa_ref)

    def att_pair(j, carry):
        kc = 2 * j
        logits(kc + 1, sb_ref)
        softmax_pv(kc, sa_ref)
        logits(jnp.minimum(kc + 2, nkc - 1), sa_ref)
        softmax_pv(kc + 1, sb_ref)
        return carry

    lax.fori_loop(0, lax.div(nkc, 2), att_pair, 0)

    @pl.when(lax.rem(nkc, 2) == 1)
    def _():
        softmax_pv(nkc - 1, sa_ref, diagonal=True)

    outs = []
    for g in range(N_ATT_HEADS // 2):
        l_lo = jnp.sum(l_ref[2 * g], axis=-1, keepdims=True)
        l_hi = jnp.sum(l_ref[2 * g + 1], axis=-1, keepdims=True)
        outs.append(acc_ref[g] * jnp.where(lo, 1.0 / l_lo, 1.0 / l_hi))
    o_ref[...] = (jnp.concatenate(outs, axis=1) * _silu(gatt_ref[...])).astype(BF16)


def _attention(q, qi, misc, gatt, kd, kid, vd, bsz, seq_len):
    m = q.shape[0]
    qb = min(512, seq_len)
    kc_w = min(512, seq_len)
    topk = min(INDEX_TOPK_MAX, seq_len // 4)
    assert seq_len % kc_w == 0 and kc_w == qb and qb % LANES == 0 and kc_w >= topk
    nqb = seq_len // qb
    qrow = lambda b, i: (b * nqb + i, 0)
    kv = lambda b, i: (b, 0)
    kern = functools.partial(_attn_kernel, qb=qb, kc_w=kc_w, topk=topk)
    tri = (jnp.arange(LANES)[:, None] <= jnp.arange(LANES)[None, :]).astype(BF16)
    return pl.pallas_call(
        kern,
        grid=(bsz, nqb),
        in_specs=[pl.BlockSpec((qb, ATT_W), qrow),
                  pl.BlockSpec((qb, N_IDX_HEADS * IDX_DIM), qrow),
                  pl.BlockSpec((qb, MISC_W), qrow),
                  pl.BlockSpec((qb, ATT_W), qrow),
                  pl.BlockSpec((seq_len, LANES), kv),
                  pl.BlockSpec((seq_len, LANES), kv),
                  pl.BlockSpec((seq_len, LANES), kv),
                  pl.BlockSpec((LANES, LANES), lambda b, i: (0, 0))],
        out_specs=pl.BlockSpec((qb, ATT_W), qrow),
        out_shape=jax.ShapeDtypeStruct((m, ATT_W), BF16),
        scratch_shapes=[pltpu.VMEM((qb // LANES, seq_len, LANES), F32),
                        pltpu.VMEM((N_ATT_HEADS * qb, LANES), BF16),
                        pltpu.VMEM((N_IDX_HEADS * qb, LANES), BF16),
                        pltpu.VMEM((N_ATT_HEADS * qb, kc_w), F32),
                        pltpu.VMEM((N_ATT_HEADS * qb, kc_w), F32),
                        pltpu.VMEM((N_ATT_HEADS * qb, kc_w), BF16),
                        pltpu.VMEM((N_ATT_HEADS, qb, LANES), F32),
                        pltpu.VMEM((N_ATT_HEADS, qb, LANES), F32),
                        pltpu.VMEM((N_ATT_HEADS // 2, qb, LANES), F32)],
        compiler_params=pltpu.CompilerParams(vmem_limit_bytes=VMEM_LIMIT,
                                             dimension_semantics=("arbitrary", "arbitrary")),
        name="dsa_attention",
    )(q, qi, misc, gatt, kd, kid, vd, tri)


def _ssd_kernel(xbc_ref, misc_ref, z_ref, cw_ref, cb_ref, dtb_ref, alog_ref, dsk_ref, ng_ref, e_ref,
                y_ref, ext_ref, st_ref):
    c = pl.program_id(1)
    gw = D_SSM // N_GROUPS
    tail = 8

    @pl.when(c == 0)
    def _():
        ext_ref[0:tail, :] = jnp.zeros((tail, CONV_CH), F32)
        st_ref[...] = jnp.zeros(st_ref.shape, F32)

    ext_ref[tail:tail + CHUNK, :] = xbc_ref[...]
    conv = cb_ref[...]
    for j in range(CONV_K):
        conv = conv + ext_ref[tail - j:tail - j + CHUNK, :] * cw_ref[CONV_K - 1 - j:CONV_K - j, :]
    ext_ref[0:tail, :] = ext_ref[CHUNK:CHUNK + tail, :]
    u = _silu(conv)
    xs = u[:, 0:D_SSM]
    bm = u[:, D_SSM:D_SSM + N_GROUPS * D_STATE]
    cm = u[:, D_SSM + N_GROUPS * D_STATE:]

    row = lax.broadcasted_iota(jnp.int32, (CHUNK, LANES), 0)
    lane = lax.broadcasted_iota(jnp.int32, (CHUNK, LANES), 1)
    tril = row >= lane
    lo = lane < SSM_HEAD_DIM

    dtr = misc_ref[...] + dtb_ref[...]
    softplus = jnp.maximum(dtr, 0.0) + jnp.log1p(jnp.exp(-jnp.abs(dtr)))
    dtv = jnp.where(lane < N_SSM_HEADS, softplus, 0.0)
    adt = dtv * (-jnp.exp(alog_ref[...]))
    acs = _dot_sel(jnp.where(tril, 1.0, 0.0).astype(BF16), adt)
    acs_t = acs.T
    expand = e_ref[...]
    dt_e = _dot_sel(dtv, expand)
    acs_e = _dot_sel(acs, expand)
    last_e = acs_e[CHUNK - 1:CHUNK, :]
    xd = xs * dt_e
    w_state = (xd * jnp.exp(last_e - acs_e)).astype(BF16)
    decay_in = jnp.exp(acs_e)
    chunk_decay = jnp.exp(last_e)

    ys = []
    for g in range(N_GROUPS):
        bm_g = bm[:, g * D_STATE:(g + 1) * D_STATE]
        cmb = cm[:, g * D_STATE:(g + 1) * D_STATE].astype(BF16)
        cb = lax.dot_general(cmb, bm_g.astype(BF16), _NT, preferred_element_type=F32)
        st = st_ref[g]
        y_off = _dot(cmb, st.astype(BF16)) * decay_in[:, g * gw:(g + 1) * gw]
        st_ref[g] = st * chunk_decay[:, g * gw:(g + 1) * gw] + _dot(bm_g.T.astype(BF16), w_state[:, g * gw:(g + 1) * gw])
        for p in range(gw // LANES):
            pair = g * (gw // LANES) + p
            xp = xd[:, pair * LANES:(pair + 1) * LANES]
            acc = y_off[:, p * LANES:(p + 1) * LANES]
            for half in range(2):
                h = 2 * pair + half
                seg = acs[:, h:h + 1] - acs_t[h:h + 1, :]
                decay = jnp.exp(jnp.where(tril, seg, -jnp.inf))
                xh = jnp.where(lo if half == 0 else jnp.logical_not(lo), xp, 0.0).astype(BF16)
                acc = acc + _dot((cb * decay).astype(BF16), xh)
            ys.append(acc)
    y = jnp.concatenate(ys, axis=1) + dsk_ref[...] * xs
    y = y * _silu(z_ref[...])
    normed = []
    for g in range(N_GROUPS):
        yg = y[:, g * gw:(g + 1) * gw]
        normed.append(yg * lax.rsqrt(jnp.mean(yg * yg, axis=-1, keepdims=True) + EPS))
    y_ref[...] = (jnp.concatenate(normed, axis=1) * ng_ref[...]).astype(BF16)


def _ssd(xbc, misc, z, conv_w, conv_b, dt_bias, a_log, d_skip, ssm_norm_g, expand, bsz, seq_len):
    m = xbc.shape[0]
    nc = seq_len // CHUNK
    rowc = lambda b, c: (b * nc + c, 0)
    const = lambda b, c: (0, 0)
    pad = lambda v: jnp.pad(v, (0, LANES - v.shape[0])).reshape(1, LANES)
    return pl.pallas_call(
        _ssd_kernel,
        grid=(bsz, nc),
        in_specs=[pl.BlockSpec((CHUNK, CONV_CH), rowc),
                  pl.BlockSpec((CHUNK, MISC_W), rowc),
                  pl.BlockSpec((CHUNK, D_SSM), rowc),
                  pl.BlockSpec((CONV_K, CONV_CH), const),
                  pl.BlockSpec((1, CONV_CH), const),
                  pl.BlockSpec((1, LANES), const),
                  pl.BlockSpec((1, LANES), const),
                  pl.BlockSpec((1, D_SSM), const),
                  pl.BlockSpec((1, D_SSM), const),
                  pl.BlockSpec((LANES, D_SSM), const)],
        out_specs=pl.BlockSpec((CHUNK, D_SSM), rowc),
        out_shape=jax.ShapeDtypeStruct((m, D_SSM), BF16),
        scratch_shapes=[pltpu.VMEM((CHUNK + 8, CONV_CH), F32),
                        pltpu.VMEM((N_GROUPS, D_STATE, D_SSM // N_GROUPS), F32)],
        compiler_params=pltpu.CompilerParams(vmem_limit_bytes=VMEM_LIMIT,
                                             dimension_semantics=("arbitrary", "arbitrary")),
        name="ssd_mixer",
    )(xbc, misc, z, conv_w, conv_b.reshape(1, CONV_CH), pad(dt_bias), pad(a_log),
      jnp.repeat(d_skip, SSM_HEAD_DIM).reshape(1, D_SSM), ssm_norm_g.reshape(1, D_SSM), expand)


def _merge_kernel(x_ref, oa_ref, y_ref, gla_ref, gls_ref, gate_ref, wa_ref, ws_ref, wo_ref, fg_ref, o_ref,
                  *, final_norm):
    y_att = _dot(oa_ref[...], wa_ref[...])
    y_ssd = _dot(y_ref[...], ws_ref[...])
    merged = jax.nn.sigmoid(gla_ref[...]) * y_att + jax.nn.sigmoid(gls_ref[...]) * y_ssd
    out = x_ref[...] + gate_ref[0] * _dot(merged.astype(BF16), wo_ref[...])
    if final_norm:
        out = out * lax.rsqrt(jnp.mean(out * out, axis=-1, keepdims=True) + EPS) * fg_ref[...]
    o_ref[...] = out


def _merge(x2, oa, y, gla, gls, gate, w_a, w_s, w_o, layer, final_g, seq_len, final_norm):
    m, d = x2.shape
    tm = 512
    nbl = seq_len // tm
    row = lambda i: (i, 0)
    const = lambda i: (0, 0)
    stacked = lambda i: (layer, 0, 0)
    return pl.pallas_call(
        functools.partial(_merge_kernel, final_norm=final_norm),
        grid=(m // tm,),
        in_specs=[pl.BlockSpec((tm, d), row),
                  pl.BlockSpec((tm, ATT_W), row),
                  pl.BlockSpec((tm, D_SSM), row),
                  pl.BlockSpec((tm, d), row),
                  pl.BlockSpec((tm, d), row),
                  pl.BlockSpec((1, 1, d), lambda i: (i // nbl, 0, 0)),
                  pl.BlockSpec((None, ATT_W, d), stacked),
                  pl.BlockSpec((None, D_SSM, d), stacked),
                  pl.BlockSpec((None, d, d), stacked),
                  pl.BlockSpec((1, d), const)],
        out_specs=pl.BlockSpec((tm, d), row),
        out_shape=jax.ShapeDtypeStruct((m, d), F32),
        compiler_params=pltpu.CompilerParams(vmem_limit_bytes=VMEM_LIMIT),
        name="merge_out_projection",
    )(x2, oa, y, gla, gls, gate, w_a, w_s, w_o, final_g)


def _relayout_w_in(w_in):
    w_in = w_in.astype(BF16)
    offs = np.cumsum((0,) + IN_SIZES)
    seg = [w_in[..., int(offs[i]):int(offs[i + 1])] for i in range(len(IN_SIZES))]
    q, k, v, g_att, qi, ki, wi, z, xbc, dt, gla, gls = seg
    zeros = lambda n: jnp.zeros(w_in.shape[:-1] + (n,), w_in.dtype)
    cols = [q, k, k, ki, ki, qi,
            v, v,
            g_att, dt, wi, zeros(MISC_W - N_SSM_HEADS - N_IDX_HEADS), z, xbc, gla, gls]
    return jnp.concatenate(cols, axis=-1)


def _rope_tables(seq_len):
    inv = ROPE_THETA ** (-jnp.arange(0, HEAD_DIM, 2, dtype=F32) / HEAD_DIM)
    ang = jnp.arange(seq_len, dtype=F32)[:, None] * inv[None, :]
    cos, sin = jnp.cos(ang), jnp.sin(ang)
    return (jnp.concatenate([cos, cos, cos, cos], axis=-1),
            jnp.concatenate([-sin, sin, -sin, sin], axis=-1))


def kernel(x, c, w_ada, b_ada, norm_g, w_in, conv_w, conv_b, dt_bias, a_log, d_skip, ssm_norm_g,
           w_branch_a, w_branch_s, w_out, final_g):
    bsz, seq_len, d = x.shape
    depth = w_in.shape[0]
    assert d == D_MODEL and seq_len % CHUNK == 0
    mod = _modulation(c, w_ada, b_ada)
    w_r = _relayout_w_in(w_in)
    w_a = w_branch_a.astype(BF16)
    w_s = w_branch_s.astype(BF16)
    w_o = w_out.astype(BF16)
    cos_t, sin_t = _rope_tables(seq_len)
    expand = (jnp.arange(LANES)[:, None] == (jnp.arange(D_SSM) // SSM_HEAD_DIM)[None, :]).astype(BF16)
    fg = final_g.reshape(1, d)

    x2 = x.reshape(bsz * seq_len, d)
    for i in range(depth):
        shift = mod[i, :, 0:d].reshape(bsz, 1, d)
        scale = mod[i, :, d:2 * d].reshape(bsz, 1, d)
        gate = mod[i, :, 2 * d:].reshape(bsz, 1, d)
        (q, kd, kid, qi, vd, gatt, misc, z, xbc, gla, gls) = _in_projection(
            x2, scale, shift, norm_g[i].reshape(1, d), w_r, i, cos_t, sin_t, seq_len)
        oa = _attention(q, qi, misc, gatt, kd, kid, vd, bsz, seq_len)
        y = _ssd(xbc, misc, z, conv_w[i], conv_b[i], dt_bias[i], a_log[i], d_skip[i], ssm_norm_g[i],
                 expand, bsz, seq_len)
        x2 = _merge(x2, oa, y, gla, gls, gate, w_a, w_s, w_o, i, fg, seq_len, i == depth - 1)
    return x2.reshape(bsz, seq_len, d)
```
